```python
import math
import jax, jax.numpy as jnp
from jax import lax
import numpy as np

D_MODEL = 4096
BATCH = 4
SEQ = 4096
DEPTH = 4

CHUNK = 64
N_MIXERS = 2
SSD_EXPAND = 2
SSD_D_INNER = SSD_EXPAND * D_MODEL
SSD_HEADDIM = 64
SSD_HEADS = SSD_D_INNER // SSD_HEADDIM
SSD_GROUPS = 8
SSD_HEADS_PER_GROUP = SSD_HEADS // SSD_GROUPS
SSD_STATE = 128
SSD_CONV = 4
SSD_CONV_DIM = SSD_D_INNER + 2 * SSD_GROUPS * SSD_STATE
SSD_PROJ = SSD_D_INNER + SSD_CONV_DIM + SSD_HEADS
SC_WIDTH = 3
PEER_HEADS = 8
PEER_NKEYS = 128
PEER_EXPERTS = PEER_NKEYS * PEER_NKEYS
PEER_TOPK = 16
PEER_HALF_DIM = 128
PEER_QUERY_DIM = 2 * PEER_HALF_DIM
PEER_TOKEN_BLOCK = 128
LN_EPS = 1e-5
RMS_EPS = 1e-5
DEEPNORM_ALPHA = (2 * DEPTH) ** 0.25
DEEPNORM_BETA = (8 * DEPTH) ** -0.25

kernel_name = "hybrid_ssd_shortconv_peer_deepnorm"


def layer_norm(h, g, b):
    hf = h.astype(jnp.float32)
    mu = jnp.mean(hf, axis=-1, keepdims=True)
    var = jnp.mean(jnp.square(hf - mu), axis=-1, keepdims=True)
    out = (hf - mu) * lax.rsqrt(var + LN_EPS) * g.astype(jnp.float32) + b.astype(jnp.float32)
    return out.astype(h.dtype)


def causal_dwconv(h, w):
    k_width = w.shape[0]
    s = h.shape[1]
    hp = jnp.pad(h, ((0, 0), (k_width - 1, 0), (0, 0)))
    out = hp[:, 0:s] * w[0]
    for k in range(1, k_width):
        out = out + hp[:, k:k + s] * w[k]
    return out


def ssd_chunked_scan(xs, dt, a, bm, cm):
    bn, s, g, j, p = xs.shape
    n = bm.shape[-1]
    nc = s // CHUNK
    f32 = jnp.float32
    xdt = (xs.astype(f32) * dt[..., None]).reshape(bn, nc, CHUNK, g, j, p)
    bc = bm.astype(f32).reshape(bn, nc, CHUNK, g, n)
    cc = cm.astype(f32).reshape(bn, nc, CHUNK, g, n)
    acs = jnp.cumsum((dt * a).reshape(bn, nc, CHUNK, g, j), axis=2)
    acs_t = jnp.moveaxis(acs, 2, -1)
    seg = acs_t[..., :, None] - acs_t[..., None, :]
    causal = jnp.tril(jnp.ones((CHUNK, CHUNK), dtype=bool))
    decay_ls = jnp.exp(jnp.where(causal, seg, -jnp.inf))
    cb = jnp.einsum('bclgn,bcsgn->bcgls', cc, bc)
    w_ls = cb[:, :, :, None] * decay_ls
    y_diag = jnp.einsum('bcgjls,bcsgjp->bclgjp', w_ls, xdt)

    def step(state, inp):
        b_k, c_k, xdt_k, acs_k = inp
        y_off = jnp.einsum('blgn,bgjpn->blgjp', c_k, state) * jnp.exp(acs_k)[..., None]
        to_end = jnp.exp(acs_k[:, -1:] - acs_k)
        new_state = state * jnp.exp(acs_k[:, -1])[..., None, None] + jnp.einsum(
            'blgn,blgjp->bgjpn', b_k, xdt_k * to_end[..., None])
        return new_state, y_off

    init = jnp.zeros((bn, g, j, p, n), f32)
    swap = lambda t: jnp.moveaxis(t, 1, 0)
    _, y_off = lax.scan(step, init, (swap(bc), swap(cc), swap(xdt), swap(acs)))
    y = y_diag + jnp.moveaxis(y_off, 0, 1)
    return y.reshape(bn, s, g, j, p)


def ssd_mixer(x, in_proj, conv_w, conv_b, dt_bias, a_log, d_skip, norm_w, out_proj):
    bn, s, _ = x.shape
    f32 = jnp.float32
    zxbcdt = x @ in_proj
    z = zxbcdt[..., :SSD_D_INNER]
    xbc = zxbcdt[..., SSD_D_INNER:SSD_D_INNER + SSD_CONV_DIM]
    dt_raw = zxbcdt[..., SSD_D_INNER + SSD_CONV_DIM:]
    xbc = jax.nn.silu(causal_dwconv(xbc, conv_w) + conv_b)
    gn = SSD_GROUPS * SSD_STATE
    xs = xbc[..., :SSD_D_INNER].reshape(bn, s, SSD_GROUPS, SSD_HEADS_PER_GROUP, SSD_HEADDIM)
    bm = xbc[..., SSD_D_INNER:SSD_D_INNER + gn].reshape(bn, s, SSD_GROUPS, SSD_STATE)
    cm = xbc[..., SSD_D_INNER + gn:].reshape(bn, s, SSD_GROUPS, SSD_STATE)
    dt = jax.nn.softplus(dt_raw.astype(f32) + dt_bias.astype(f32))
    dt = dt.reshape(bn, s, SSD_GROUPS, SSD_HEADS_PER_GROUP)
    a = -jnp.exp(a_log.astype(f32)).reshape(SSD_GROUPS, SSD_HEADS_PER_GROUP)
    y = ssd_chunked_scan(xs, dt, a, bm, cm)
    y = y + d_skip.astype(f32).reshape(SSD_GROUPS, SSD_HEADS_PER_GROUP, 1) * xs.astype(f32)
    h = (y.reshape(bn, s, SSD_D_INNER) * jax.nn.silu(z.astype(f32))).reshape(bn, s, SSD_GROUPS, -1)
    h = h * lax.rsqrt(jnp.mean(jnp.square(h), axis=-1, keepdims=True) + RMS_EPS)
    h = h.reshape(bn, s, SSD_D_INNER) * norm_w.astype(f32)
    return h.astype(x.dtype) @ out_proj


def shortconv_mixer(x, in_proj, conv_w, out_proj):
    bch = x @ in_proj
    gate_b = bch[..., :D_MODEL]
    gate_c = bch[..., D_MODEL:2 * D_MODEL]
    h = bch[..., 2 * D_MODEL:]
    y = gate_b * causal_dwconv(gate_c * h, conv_w)
    return y @ out_proj


def peer_ffn(x, wq, subkeys, u, v):
    bn, s, d = x.shape
    t = bn * s
    xt = x.reshape(t, d)
    q = (xt @ wq).reshape(t, PEER_HEADS, 2, PEER_HALF_DIM).astype(jnp.float32)
    sc = jnp.einsum('thid,hikd->thik', q, subkeys.astype(jnp.float32))
    s1, i1 = lax.top_k(sc[:, :, 0], PEER_TOPK)
    s2, i2 = lax.top_k(sc[:, :, 1], PEER_TOPK)
    cand = (s1[..., :, None] + s2[..., None, :]).reshape(t, PEER_HEADS, PEER_TOPK * PEER_TOPK)
    cidx = (i1[..., :, None] * PEER_NKEYS + i2[..., None, :]).reshape(t, PEER_HEADS, PEER_TOPK * PEER_TOPK)
    top_s, pos = lax.top_k(cand, PEER_TOPK)
    eidx = jnp.take_along_axis(cidx, pos, axis=-1)
    gates = jax.nn.softmax(top_s, axis=-1).astype(x.dtype)

    nb = t // PEER_TOKEN_BLOCK

    def block(args):
        xb, eb, gb = args
        ub = jnp.take(u, eb, axis=0)
        act = jax.nn.gelu(jnp.einsum('td,thkd->thk', xb, ub), approximate=False) * gb
        vb = jnp.take(v, eb, axis=0)
        return jnp.einsum('thk,thkd->td', act, vb)

    out = lax.map(block, (xt.reshape(nb, PEER_TOKEN_BLOCK, d),
                          eidx.reshape(nb, PEER_TOKEN_BLOCK, PEER_HEADS, PEER_TOPK),
                          gates.reshape(nb, PEER_TOKEN_BLOCK, PEER_HEADS, PEER_TOPK)))
    return out.reshape(bn, s, d)


def setup_inputs(seed: int = 0) -> dict:
    key = jax.random.key(seed)
    ks = jax.random.split(key, 24)
    n_ssd = (DEPTH + 1) // 2
    n_sc = DEPTH // 2
    nrm = jax.random.normal
    f32 = jnp.float32
    x = nrm(ks[0], (BATCH, SEQ, D_MODEL), f32)
    ssd_in_proj = nrm(ks[1], (n_ssd, D_MODEL, SSD_PROJ), f32) * D_MODEL ** -0.5
    ssd_conv_w = nrm(ks[2], (n_ssd, SSD_CONV, SSD_CONV_DIM), f32) * SSD_CONV ** -0.5
    ssd_conv_b = nrm(ks[3], (n_ssd, SSD_CONV_DIM), f32) * 0.01
    dt0 = jnp.exp(jax.random.uniform(ks[4], (n_ssd, SSD_HEADS), f32)
                  * (math.log(0.1) - math.log(0.001)) + math.log(0.001))
    ssd_dt_bias = dt0 + jnp.log(-jnp.expm1(-dt0))
    ssd_A_log = jnp.log(jax.random.uniform(ks[5], (n_ssd, SSD_HEADS), f32, minval=1.0, maxval=16.0))
    ssd_D = 1.0 + 0.1 * nrm(ks[6], (n_ssd, SSD_HEADS), f32)
    ssd_norm_w = 1.0 + 0.1 * nrm(ks[7], (n_ssd, SSD_D_INNER), f32)
    ssd_out_proj = nrm(ks[8], (n_ssd, SSD_D_INNER, D_MODEL), f32) * (SSD_D_INNER ** -0.5 * DEEPNORM_BETA)
    sc_in_proj = nrm(ks[9], (n_sc, D_MODEL, 3 * D_MODEL), f32) * D_MODEL ** -0.5
    sc_conv_w = nrm(ks[10], (n_sc, SC_WIDTH, D_MODEL), f32) * SC_WIDTH ** -0.5
    sc_out_proj = nrm(ks[11], (n_sc, D_MODEL, D_MODEL), f32) * (D_MODEL ** -0.5 * DEEPNORM_BETA)
    peer_wq = nrm(ks[12], (DEPTH, D_MODEL, PEER_HEADS * PEER_QUERY_DIM), f32) * D_MODEL ** -0.5
    peer_subkeys = nrm(ks[13], (DEPTH, PEER_HEADS, 2, PEER_NKEYS, PEER_HALF_DIM), f32) * PEER_HALF_DIM ** -0.5
    peer_u = nrm(ks[14], (DEPTH, PEER_EXPERTS, D_MODEL), f32) * D_MODEL ** -0.5
    peer_v = nrm(ks[15], (DEPTH, PEER_EXPERTS, D_MODEL), f32) * (DEEPNORM_BETA * PEER_HEADS ** -0.5)
    ln1_g = 1.0 + 0.1 * nrm(ks[16], (DEPTH, D_MODEL), f32)
    ln1_b = 0.01 * nrm(ks[17], (DEPTH, D_MODEL), f32)
    ln2_g = 1.0 + 0.1 * nrm(ks[18], (DEPTH, D_MODEL), f32)
    ln2_b = 0.01 * nrm(ks[19], (DEPTH, D_MODEL), f32)
    return {'x': x, 'ssd_in_proj': ssd_in_proj, 'ssd_conv_w': ssd_conv_w, 'ssd_conv_b': ssd_conv_b,
            'ssd_dt_bias': ssd_dt_bias, 'ssd_A_log': ssd_A_log, 'ssd_D': ssd_D, 'ssd_norm_w': ssd_norm_w,
            'ssd_out_proj': ssd_out_proj, 'sc_in_proj': sc_in_proj, 'sc_conv_w': sc_conv_w,
            'sc_out_proj': sc_out_proj, 'peer_wq': peer_wq, 'peer_subkeys': peer_subkeys,
            'peer_u': peer_u, 'peer_v': peer_v, 'ln1_g': ln1_g, 'ln1_b': ln1_b,
            'ln2_g': ln2_g, 'ln2_b': ln2_b}


def reference(x, ssd_in_proj, ssd_conv_w, ssd_conv_b, ssd_dt_bias, ssd_A_log, ssd_D, ssd_norm_w,
              ssd_out_proj, sc_in_proj, sc_conv_w, sc_out_proj, peer_wq, peer_subkeys,
              peer_u, peer_v, ln1_g, ln1_b, ln2_g, ln2_b):
    for i in range(DEPTH):
        j = i // N_MIXERS
        if i % N_MIXERS == 0:
            mix = ssd_mixer(x, ssd_in_proj[j], ssd_conv_w[j], ssd_conv_b[j], ssd_dt_bias[j],
                            ssd_A_log[j], ssd_D[j], ssd_norm_w[j], ssd_out_proj[j])
        else:
            mix = shortconv_mixer(x, sc_in_proj[j], sc_conv_w[j], sc_out_proj[j])
        x = layer_norm(DEEPNORM_ALPHA * x + mix, ln1_g[i], ln1_b[i])
        ffn = peer_ffn(x, peer_wq[i], peer_subkeys[i], peer_u[i], peer_v[i])
        x = layer_norm(DEEPNORM_ALPHA * x + ffn, ln2_g[i], ln2_b[i])
    return x
```

```python
import functools
import math

import jax
import jax.numpy as jnp
from jax import lax
from jax.experimental import pallas as pl
from jax.experimental.pallas import tpu as pltpu

F32 = jnp.float32
BF16 = jnp.bfloat16

CHUNK = 64
SSD_HEADDIM = 64
SSD_GROUPS = 8
SSD_STATE = 128
PEER_HEADS = 8
PEER_NKEYS = 128
PEER_TOPK = 16
PEER_HALF_DIM = 128
LN_EPS = 1e-5
RMS_EPS = 1e-5

VMEM_LIMIT_BYTES = 56 * 1024 * 1024


def _cparams(*sem):
    return pltpu.CompilerParams(dimension_semantics=sem, vmem_limit_bytes=VMEM_LIMIT_BYTES)


def _tile(n, pref):
    t = min(pref, n)
    while n % t:
        t //= 2
    return t


def _split3(x):
    hi = x.astype(BF16)
    r1 = x - hi.astype(F32)
    mid = r1.astype(BF16)
    lo = (r1 - mid.astype(F32)).astype(BF16)
    return hi, mid, lo


def _dot(a, b):
    return jnp.dot(a, b, preferred_element_type=F32)


def _dot3(x, sel):
    hi, mid, lo = _split3(x)
    return _dot(hi, sel) + _dot(mid, sel) + _dot(lo, sel)


def _mm_kernel(a_ref, w_ref, o_ref):
    o_ref[...] = _dot(a_ref[...], w_ref[...]).astype(o_ref.dtype)


def _matmul(a, w, out_dtype, tm, tn):
    m, k = a.shape
    n = w.shape[1]
    tm, tn = _tile(m, tm), _tile(n, tn)
    return pl.pallas_call(
        _mm_kernel,
        grid=(m // tm, n // tn),
        in_specs=[pl.BlockSpec((tm, k), lambda i, j: (i, 0)),
                  pl.BlockSpec((k, tn), lambda i, j: (0, j))],
        out_specs=pl.BlockSpec((tm, tn), lambda i, j: (i, j)),
        out_shape=jax.ShapeDtypeStruct((m, n), out_dtype),
        compiler_params=_cparams("parallel", "arbitrary"),
        name="matmul",
    )(a, w)


def _ln_kernel(x_ref, y_ref, g_ref, b_ref, o32_ref, o16_ref, *, alpha):
    h = alpha * x_ref[...] + y_ref[...].astype(F32)
    mu = jnp.mean(h, axis=-1, keepdims=True)
    d = h - mu
    var = jnp.mean(d * d, axis=-1, keepdims=True)
    out = d * lax.rsqrt(var + LN_EPS) * g_ref[...] + b_ref[...]
    o32_ref[...] = out
    o16_ref[...] = out.astype(BF16)


def _residual_ln(x, y, g, b, alpha, tm=256):
    t, d = x.shape
    tm = min(tm, t)
    row = pl.BlockSpec((tm, d), lambda i: (i, 0))
    vec = pl.BlockSpec((1, d), lambda i: (0, 0))
    return pl.pallas_call(
        functools.partial(_ln_kernel, alpha=alpha),
        grid=(t // tm,),
        in_specs=[row, row, vec, vec],
        out_specs=[row, row],
        out_shape=[jax.ShapeDtypeStruct((t, d), F32), jax.ShapeDtypeStruct((t, d), BF16)],
        compiler_params=_cparams("parallel"),
        name="residual_ln",
    )(x, y, g.reshape(1, d).astype(F32), b.reshape(1, d).astype(F32))


def _shift_rows(h, k):
    rolled = pltpu.roll(h, k, 0)
    rows = lax.broadcasted_iota(jnp.int32, h.shape, 0)
    return jnp.where(rows >= k, rolled, 0.0)


def _ssd_conv_kernel(x_ref, w_ref, b_ref, o_ref):
    h = x_ref[...].astype(F32)
    w = w_ref[...]
    kw = w.shape[0]
    acc = h * w[kw - 1:kw, :] + b_ref[...]
    for k in range(kw - 1):
        acc = acc + _shift_rows(h, kw - 1 - k) * w[k:k + 1, :]
    o_ref[...] = (acc * jax.nn.sigmoid(acc)).astype(o_ref.dtype)


def _ssd_conv(zx, conv_w, conv_b, batch, seq, d_inner, tc=128):
    t = zx.shape[0]
    cdim = conv_w.shape[1]
    off = d_inner // tc
    return pl.pallas_call(
        _ssd_conv_kernel,
        grid=(batch, cdim // tc),
        in_specs=[pl.BlockSpec((seq, tc), lambda b, j: (b, off + j)),
                  pl.BlockSpec((conv_w.shape[0], tc), lambda b, j: (0, j)),
                  pl.BlockSpec((1, tc), lambda b, j: (0, j))],
        out_specs=pl.BlockSpec((seq, tc), lambda b, j: (b, j)),
        out_shape=jax.ShapeDtypeStruct((t, cdim), BF16),
        compiler_params=_cparams("parallel", "parallel"),
        name="ssd_conv_silu",
    )(zx, conv_w.astype(F32), conv_b.reshape(1, cdim).astype(F32))


def _ssd_scan_kernel(xs_ref, b_ref, c_ref, z_ref, dtraw_ref, dtb_ref, a_ref, sel_ref,
                     dskip_ref, nw_ref, o_ref, state_ref, dte_ref, acse_ref, *, ts, gw):
    nchunk = ts // CHUNK

    @pl.when(pl.program_id(2) == 0)
    def _():
        state_ref[...] = jnp.zeros_like(state_ref)

    xr = dtraw_ref[...] + dtb_ref[...]
    dt = jnp.maximum(xr, 0.0) + jnp.log1p(jnp.exp(-jnp.abs(xr)))
    da = dt * a_ref[...]
    ri = lax.broadcasted_iota(jnp.int32, (ts, ts), 0)
    ci = lax.broadcasted_iota(jnp.int32, (ts, ts), 1)
    tril = ((ri >= ci) & (ri // CHUNK == ci // CHUNK)).astype(BF16)
    h1, h2, h3 = _split3(da)
    acs = _dot(tril, h1) + _dot(tril, h2) + _dot(tril, h3)
    sel = sel_ref[...]
    dte_ref[...] = _dot3(dt, sel)
    acse_ref[...] = _dot3(acs, sel)

    li = lax.broadcasted_iota(jnp.int32, (CHUNK, gw), 0)
    si = lax.broadcasted_iota(jnp.int32, (CHUNK, gw), 1) % CHUNK
    diag = li == si
    causal = li >= si
    r2 = lax.broadcasted_iota(jnp.int32, (2 * CHUNK, 2 * CHUNK), 0) // CHUNK
    c2 = lax.broadcasted_iota(jnp.int32, (2 * CHUNK, 2 * CHUNK), 1) // CHUNK
    blockdiag = r2 == c2
    dskip = dskip_ref[...]
    nw = nw_ref[...]

    def chunk(c, carry):
        rows = pl.ds(pl.multiple_of(c * CHUNK, CHUNK), CHUNK)
        cc = c_ref[rows, :]
        bc = b_ref[rows, :]
        xs = xs_ref[rows, :].astype(F32)
        dte = dte_ref[rows, :]
        acse = acse_ref[rows, :]
        xdt = xs * dte
        acs_last = acse[CHUNK - 1:CHUNK, :]
        acs_t = jnp.sum(jnp.where(diag, acse, 0.0), axis=0, keepdims=True)
        decay = jnp.exp(jnp.where(causal, acse - acs_t, -jnp.inf))
        bdup = jnp.concatenate([bc, bc], axis=0)
        cb2 = lax.dot_general(cc, bdup, (((1,), (1,)), ((), ())), preferred_element_type=F32)
        xdt16 = xdt.astype(BF16)
        ys = []
        for q in range(gw // (2 * CHUNK)):
            lanes = slice(q * 2 * CHUNK, (q + 1) * 2 * CHUNK)
            wq = (cb2 * decay[:, lanes]).astype(BF16)
            xq = xdt16[:, lanes]
            bd = jnp.where(blockdiag, jnp.concatenate([xq, xq], axis=0), jnp.zeros((), BF16))
            ys.append(_dot(wq, bd))
        y = jnp.concatenate(ys, axis=1) if len(ys) > 1 else ys[0]
        state = state_ref[...]
        y = y + _dot(cc, state.astype(BF16)) * jnp.exp(acse)
        xdtw = (xdt * jnp.exp(acs_last - acse)).astype(BF16)
        state_ref[...] = state * jnp.exp(acs_last) + lax.dot_general(
            bc, xdtw, (((0,), (0,)), ((), ())), preferred_element_type=F32)
        y = y + dskip * xs
        zc = z_ref[rows, :].astype(F32)
        hh = y * (zc * jax.nn.sigmoid(zc))
        ms = jnp.mean(hh * hh, axis=-1, keepdims=True)
        o_ref[rows, :] = (hh * lax.rsqrt(ms + RMS_EPS) * nw).astype(o_ref.dtype)
        return carry

    lax.fori_loop(0, nchunk, chunk, 0)


def _ssd_scan(zx, xbc, dt_raw, dt_bias, a_log, d_skip, norm_w, batch, seq, d_inner, ts=512):
    t = zx.shape[0]
    nheads = dt_raw.shape[1]
    gw = d_inner // SSD_GROUPS
    hpg = nheads // SSD_GROUPS
    assert gw % (2 * CHUNK) == 0 and gw == hpg * SSD_HEADDIM
    ts = min(ts, seq)
    spb = seq // ts
    nb = SSD_STATE // 128
    head_of_lane = jnp.arange(d_inner, dtype=jnp.int32) // SSD_HEADDIM
    sel = (jnp.arange(nheads, dtype=jnp.int32)[:, None] == head_of_lane[None, :]).astype(BF16)
    dskip = jnp.repeat(d_skip.astype(F32), SSD_HEADDIM).reshape(1, d_inner)
    a = -jnp.exp(a_log.astype(F32)).reshape(1, nheads)
    kern = functools.partial(_ssd_scan_kernel, ts=ts, gw=gw)
    goff = d_inner // gw
    boff = d_inner // SSD_STATE
    return pl.pallas_call(
        kern,
        grid=(batch, SSD_GROUPS, spb),
        in_specs=[
            pl.BlockSpec((ts, gw), lambda b, g, s: (b * spb + s, g)),
            pl.BlockSpec((ts, SSD_STATE), lambda b, g, s: (b * spb + s, boff + g * nb)),
            pl.BlockSpec((ts, SSD_STATE), lambda b, g, s: (b * spb + s, boff + (SSD_GROUPS + g) * nb)),
            pl.BlockSpec((ts, gw), lambda b, g, s: (b * spb + s, g)),
            pl.BlockSpec((ts, nheads), lambda b, g, s: (b * spb + s, 0)),
            pl.BlockSpec((1, nheads), lambda b, g, s: (0, 0)),
            pl.BlockSpec((1, nheads), lambda b, g, s: (0, 0)),
            pl.BlockSpec((nheads, gw), lambda b, g, s: (0, g)),
            pl.BlockSpec((1, gw), lambda b, g, s: (0, g)),
            pl.BlockSpec((1, gw), lambda b, g, s: (0, g)),
        ],
        out_specs=pl.BlockSpec((ts, gw), lambda b, g, s: (b * spb + s, g)),
        out_shape=jax.ShapeDtypeStruct((t, d_inner), BF16),
        scratch_shapes=[pltpu.VMEM((SSD_STATE, gw), F32),
                        pltpu.VMEM((ts, gw), F32),
                        pltpu.VMEM((ts, gw), F32)],
        compiler_params=_cparams("parallel", "parallel", "arbitrary"),
        name="ssd_scan",
    )(xbc, xbc, xbc, zx, dt_raw, dt_bias.reshape(1, nheads).astype(F32), a, sel, dskip,
      norm_w.reshape(1, d_inner).astype(F32))


def _ssd_mixer(x16, in_proj, conv_w, conv_b, dt_bias, a_log, d_skip, norm_w, out_proj, batch, seq):
    d_inner = out_proj.shape[0]
    cdim = conv_w.shape[1]
    w16 = in_proj.astype(BF16)
    zx = _matmul(x16, w16[:, :d_inner + cdim], BF16, 1024, 1024)
    dt_raw = _matmul(x16, w16[:, d_inner + cdim:], F32, 1024, 128)
    xbc = _ssd_conv(zx, conv_w, conv_b, batch, seq, d_inner)
    h = _ssd_scan(zx, xbc, dt_raw, dt_bias, a_log, d_skip, norm_w, batch, seq, d_inner)
    return _matmul(h, out_proj.astype(BF16), F32, 1024, 512)


def _sc_gate_kernel(gb_ref, gc_ref, h_ref, w_ref, o_ref):
    u = gc_ref[...].astype(F32) * h_ref[...].astype(F32)
    w = w_ref[...]
    kw = w.shape[0]
    acc = u * w[kw - 1:kw, :]
    for k in range(kw - 1):
        acc = acc + _shift_rows(u, kw - 1 - k) * w[k:k + 1, :]
    o_ref[...] = (gb_ref[...].astype(F32) * acc).astype(o_ref.dtype)


def _sc_gate(bch, conv_w, batch, seq, d, tc=128):
    t = bch.shape[0]
    nj = d // tc
    return pl.pallas_call(
        _sc_gate_kernel,
        grid=(batch, nj),
        in_specs=[pl.BlockSpec((seq, tc), lambda b, j: (b, j)),
                  pl.BlockSpec((seq, tc), lambda b, j: (b, nj + j)),
                  pl.BlockSpec((seq, tc), lambda b, j: (b, 2 * nj + j)),
                  pl.BlockSpec((conv_w.shape[0], tc), lambda b, j: (0, j))],
        out_specs=pl.BlockSpec((seq, tc), lambda b, j: (b, j)),
        out_shape=jax.ShapeDtypeStruct((t, d), BF16),
        compiler_params=_cparams("parallel", "parallel"),
        name="shortconv_gate",
    )(bch, bch, bch, conv_w.astype(F32))


def _shortconv_mixer(x16, in_proj, conv_w, out_proj, batch, seq):
    d = x16.shape[1]
    bch = _matmul(x16, in_proj.astype(BF16), BF16, 1024, 1024)
    y = _sc_gate(bch, conv_w, batch, seq, d)
    return _matmul(y, out_proj.astype(BF16), F32, 1024, 1024)


def _top_values(s, k):
    vals = []
    cur = s
    for _ in range(k):
        m = jnp.max(cur, axis=0, keepdims=True)
        vals.append(m)
        cur = jnp.where(cur == m, -jnp.inf, cur)
    return jnp.concatenate(vals, axis=0)


def _peer_topk_kernel(qt_ref, keys_ref, s1_ref, s2_ref, tau_ref, cst_ref):
    dh = PEER_HALF_DIM

    def head(h, carry):
        r0 = pl.multiple_of(h * 2 * dh, 2 * dh)
        s1 = _dot(keys_ref[h, 0], qt_ref[pl.ds(r0, dh), :])
        s2 = _dot(keys_ref[h, 1], qt_ref[pl.ds(r0 + dh, dh), :])
        s1_ref[h] = s1
        s2_ref[h] = s2
        sa = _top_values(s1, PEER_TOPK)
        sb = _top_values(s2, PEER_TOPK)
        pieces = [sa[0:1, :] + sb]
        for r in range(1, 8):
            pieces.append(sa[r:r + 1, :] + sb[0:8, :])
        pieces.append(sa[8:16, :] + sb[0:1, :])
        top = _top_values(jnp.concatenate(pieces, axis=0), PEER_TOPK)
        mx = top[0:1, :]
        z = jnp.sum(jnp.exp(top - mx), axis=0, keepdims=True)
        tau_ref[h] = top[PEER_TOPK - 1:PEER_TOPK, :]
        cst_ref[h] = mx + jnp.log(z)
        return carry

    lax.fori_loop(0, PEER_HEADS, head, 0)


def _peer_topk(qt, keys16, tm=256):
    t = qt.shape[1]
    tm = min(tm, t)
    sc_shape = jax.ShapeDtypeStruct((PEER_HEADS, PEER_NKEYS, t), F32)
    row_shape = jax.ShapeDtypeStruct((PEER_HEADS, 1, t), F32)
    sc_spec = pl.BlockSpec((PEER_HEADS, PEER_NKEYS, tm), lambda i: (0, 0, i))
    row_spec = pl.BlockSpec((PEER_HEADS, 1, tm), lambda i: (0, 0, i))
    return pl.pallas_call(
        _peer_topk_kernel,
        grid=(t // tm,),
        in_specs=[pl.BlockSpec((qt.shape[0], tm), lambda i: (0, i)),
                  pl.BlockSpec(keys16.shape, lambda i: (0, 0, 0, 0))],
        out_specs=[sc_spec, sc_spec, row_spec, row_spec],
        out_shape=[sc_shape, sc_shape, row_shape, row_shape],
        compiler_params=_cparams("parallel"),
        name="peer_topk",
    )(qt, keys16)


def _peer_dense_kernel(xt_ref, u_ref, vt_ref, s1_ref, s2_ref, tau_ref, cst_ref, o_ref,
                       h_ref, act_ref, *, te, rb):
    e = pl.program_id(1)
    nk = PEER_NKEYS

    @pl.when(e == 0)
    def _():
        o_ref[...] = jnp.zeros_like(o_ref)

    h_ref[...] = _dot(u_ref[...], xt_ref[...])

    def rows_body(rc, carry):
        i_glob = e * (te // nk) + rc // (nk // rb)
        j0 = pl.multiple_of((rc % (nk // rb)) * rb, rb)
        gate = None
        for h in range(PEER_HEADS):
            s = s1_ref[h, pl.ds(i_glob, 1), :] + s2_ref[h, pl.ds(j0, rb), :]
            g = jnp.where(s >= tau_ref[h], jnp.exp(s - cst_ref[h]), 0.0)
            gate = g if gate is None else gate + g
        rows = pl.ds(pl.multiple_of(rc * rb, rb), rb)
        pre = h_ref[rows, :]
        act = 0.5 * pre * (1.0 + lax.erf(pre * (1.0 / math.sqrt(2.0))))
        act_ref[rows, :] = (act * gate).astype(BF16)
        return carry

    lax.fori_loop(0, te // rb, rows_body, 0)
    o_ref[...] += _dot(vt_ref[...], act_ref[...])


def _peer_dense(xt, u16, vt16, s1, s2, tau, cst, tm=512, te=512, rb=16):
    d, t = xt.shape
    ne = u16.shape[0]
    tm, te = min(tm, t), min(te, ne)
    assert te % PEER_NKEYS == 0 and PEER_NKEYS % rb == 0
    sc_spec = pl.BlockSpec((PEER_HEADS, PEER_NKEYS, tm), lambda i, e: (0, 0, i))
    row_spec = pl.BlockSpec((PEER_HEADS, 1, tm), lambda i, e: (0, 0, i))
    return pl.pallas_call(
        functools.partial(_peer_dense_kernel, te=te, rb=rb),
        grid=(t // tm, ne // te),
        in_specs=[pl.BlockSpec((d, tm), lambda i, e: (0, i)),
                  pl.BlockSpec((te, d), lambda i, e: (e, 0)),
                  pl.BlockSpec((d, te), lambda i, e: (0, e)),
                  sc_spec, sc_spec, row_spec, row_spec],
        out_specs=pl.BlockSpec((d, tm), lambda i, e: (0, i)),
        out_shape=jax.ShapeDtypeStruct((d, t), F32),
        scratch_shapes=[pltpu.VMEM((te, tm), F32), pltpu.VMEM((te, tm), BF16)],
        compiler_params=_cparams("parallel", "arbitrary"),
        name="peer_dense",
    )(xt, u16, vt16, s1, s2, tau, cst)


def _peer_ffn(x16, wq, subkeys, u, v):
    xt = x16.T
    qt = _matmul(wq.T.astype(BF16), xt, BF16, 1024, 1024)
    s1, s2, tau, cst = _peer_topk(qt, subkeys.astype(BF16))
    out_t = _peer_dense(xt, u.astype(BF16), v.T.astype(BF16), s1, s2, tau, cst)
    return out_t.T


def kernel(x, ssd_in_proj, ssd_conv_w, ssd_conv_b, ssd_dt_bias, ssd_A_log, ssd_D, ssd_norm_w,
           ssd_out_proj, sc_in_proj, sc_conv_w, sc_out_proj, peer_wq, peer_subkeys, peer_u,
           peer_v, ln1_g, ln1_b, ln2_g, ln2_b):
    batch, seq, d = x.shape
    depth = peer_wq.shape[0]
    alpha = (2 * depth) ** 0.25
    x32 = x.reshape(batch * seq, d).astype(F32)
    x16 = x32.astype(BF16)
    for i in range(depth):
        j = i // 2
        if i % 2 == 0:
            mix = _ssd_mixer(x16, ssd_in_proj[j], ssd_conv_w[j], ssd_conv_b[j], ssd_dt_bias[j],
                             ssd_A_log[j], ssd_D[j], ssd_norm_w[j], ssd_out_proj[j], batch, seq)
        else:
            mix = _shortconv_mixer(x16, sc_in_proj[j], sc_conv_w[j], sc_out_proj[j], batch, seq)
        x32, x16 = _residual_ln(x32, mix, ln1_g[i], ln1_b[i], alpha)
        ffn = _peer_ffn(x16, peer_wq[i], peer_subkeys[i], peer_u[i], peer_v[i])
        x32, x16 = _residual_ln(x32, ffn, ln2_g[i], ln2_b[i], alpha)
    return x32.reshape(batch, seq, d).astype(x.dtype)
```

```python
import functools
import math

import jax
import jax.numpy as jnp
from jax import lax
from jax.experimental import pallas as pl
from jax.experimental.pallas import tpu as pltpu

F32 = jnp.float32
BF16 = jnp.bfloat16

CHUNK = 64
SSD_HEADDIM = 64
SSD_GROUPS = 8
SSD_STATE = 128
PEER_HEADS = 8
PEER_NKEYS = 128
PEER_TOPK = 16
PEER_HALF_DIM = 128
LN_EPS = 1e-5
RMS_EPS = 1e-5
LOG2E = 1.4426950408889634

SUBLANES = 8
VMEM_LIMIT_BYTES = 56 * 1024 * 1024


def _cparams(*sem):
    return pltpu.CompilerParams(dimension_semantics=sem, vmem_limit_bytes=VMEM_LIMIT_BYTES)


def _tile(n, pref):
    t = min(pref, n)
    while n % t:
        t //= 2
    return t


def _split3(x):
    hi = x.astype(BF16)
    r1 = x - hi.astype(F32)
    mid = r1.astype(BF16)
    lo = (r1 - mid.astype(F32)).astype(BF16)
    return hi, mid, lo


def _dot(a, b):
    return jnp.dot(a, b, preferred_element_type=F32)


def _dot3(x, sel):
    hi, mid, lo = _split3(x)
    return _dot(hi, sel) + _dot(mid, sel) + _dot(lo, sel)


def _mm_kernel(a_ref, w_ref, o_ref):
    o_ref[...] = _dot(a_ref[...], w_ref[...]).astype(o_ref.dtype)


def _matmul(a, w, out_dtype, tm, tn):
    m, k = a.shape
    n = w.shape[1]
    tm, tn = _tile(m, tm), _tile(n, tn)
    return pl.pallas_call(
        _mm_kernel,
        grid=(m // tm, n // tn),
        in_specs=[pl.BlockSpec((tm, k), lambda i, j: (i, 0)),
                  pl.BlockSpec((k, tn), lambda i, j: (0, j))],
        out_specs=pl.BlockSpec((tm, tn), lambda i, j: (i, j)),
        out_shape=jax.ShapeDtypeStruct((m, n), out_dtype),
        compiler_params=_cparams("parallel", "arbitrary"),
        name="matmul",
    )(a, w)


def _ln_kernel(x_ref, y_ref, g_ref, b_ref, o32_ref, o16_ref, *, alpha, transposed):
    y = y_ref[...].astype(F32)
    if transposed:
        y = y.T
    h = alpha * x_ref[...] + y
    mu = jnp.mean(h, axis=-1, keepdims=True)
    d = h - mu
    var = jnp.mean(d * d, axis=-1, keepdims=True)
    out = d * lax.rsqrt(var + LN_EPS) * g_ref[...] + b_ref[...]
    o32_ref[...] = out
    if transposed:
        o16_ref[...] = out.astype(BF16)
    else:
        o16_ref[...] = out.T.astype(BF16)


def _residual_ln(x, y, g, b, alpha, transposed, tm=256):
    t, d = x.shape
    tm = _tile(t, tm)
    row = pl.BlockSpec((tm, d), lambda i: (i, 0))
    col = pl.BlockSpec((d, tm), lambda i: (0, i))
    vec = pl.BlockSpec((1, d), lambda i: (0, 0))
    return pl.pallas_call(
        functools.partial(_ln_kernel, alpha=alpha, transposed=transposed),
        grid=(t // tm,),
        in_specs=[row, col if transposed else row, vec, vec],
        out_specs=[row, row if transposed else col],
        out_shape=[jax.ShapeDtypeStruct((t, d), F32),
                   jax.ShapeDtypeStruct((t, d) if transposed else (d, t), BF16)],
        compiler_params=_cparams("parallel"),
        name="residual_ln",
    )(x, y, g.reshape(1, d).astype(F32), b.reshape(1, d).astype(F32))


def _shift_rows(h, k):
    rolled = pltpu.roll(h, k, 0)
    rows = lax.broadcasted_iota(jnp.int32, h.shape, 0)
    return jnp.where(rows >= k, rolled, 0.0)


def _ssd_conv_kernel(x_ref, w_ref, b_ref, o_ref):
    h = x_ref[...].astype(F32)
    w = w_ref[...]
    kw = w.shape[0]
    acc = h * w[kw - 1:kw, :] + b_ref[...]
    for k in range(kw - 1):
        acc = acc + _shift_rows(h, kw - 1 - k) * w[k:k + 1, :]
    o_ref[...] = (acc * jax.nn.sigmoid(acc)).astype(o_ref.dtype)


def _ssd_conv(zx, conv_w, conv_b, batch, seq, d_inner, tc=128):
    t = zx.shape[0]
    cdim = conv_w.shape[1]
    off = d_inner // tc
    return pl.pallas_call(
        _ssd_conv_kernel,
        grid=(batch, cdim // tc),
        in_specs=[pl.BlockSpec((seq, tc), lambda b, j: (b, off + j)),
                  pl.BlockSpec((conv_w.shape[0], tc), lambda b, j: (0, j)),
                  pl.BlockSpec((1, tc), lambda b, j: (0, j))],
        out_specs=pl.BlockSpec((seq, tc), lambda b, j: (b, j)),
        out_shape=jax.ShapeDtypeStruct((t, cdim), BF16),
        compiler_params=_cparams("parallel", "parallel"),
        name="ssd_conv_silu",
    )(zx, conv_w.astype(F32), conv_b.reshape(1, cdim).astype(F32))


def _ssd_scan_kernel(xs_ref, b_ref, c_ref, z_ref, dtraw_ref, dtb_ref, a_ref, sel_ref,
                     dskip_ref, nw_ref, o_ref, state_ref, dte_ref, acse_ref, *, ts, gw):
    nchunk = ts // CHUNK

    @pl.when(pl.program_id(2) == 0)
    def _():
        state_ref[...] = jnp.zeros_like(state_ref)

    xr = dtraw_ref[...] + dtb_ref[...]
    dt = jnp.maximum(xr, 0.0) + jnp.log1p(jnp.exp(-jnp.abs(xr)))
    da = dt * a_ref[...]
    ri = lax.broadcasted_iota(jnp.int32, (ts, ts), 0)
    ci = lax.broadcasted_iota(jnp.int32, (ts, ts), 1)
    tril = ((ri >= ci) & (ri // CHUNK == ci // CHUNK)).astype(BF16)
    h1, h2, h3 = _split3(da)
    acs = _dot(tril, h1) + _dot(tril, h2) + _dot(tril, h3)
    sel = sel_ref[...]
    dte_ref[...] = _dot3(dt, sel)
    acse_ref[...] = _dot3(acs, sel)

    li = lax.broadcasted_iota(jnp.int32, (CHUNK, gw), 0)
    si = lax.broadcasted_iota(jnp.int32, (CHUNK, gw), 1) % CHUNK
    diag = li == si
    causal = li >= si
    r2 = lax.broadcasted_iota(jnp.int32, (2 * CHUNK, 2 * CHUNK), 0) // CHUNK
    c2 = lax.broadcasted_iota(jnp.int32, (2 * CHUNK, 2 * CHUNK), 1) // CHUNK
    blockdiag = r2 == c2
    dskip = dskip_ref[...]
    nw = nw_ref[...]

    def chunk(c, carry):
        rows = pl.ds(pl.multiple_of(c * CHUNK, CHUNK), CHUNK)
        cc = c_ref[rows, :]
        bc = b_ref[rows, :]
        xs = xs_ref[rows, :].astype(F32)
        dte = dte_ref[rows, :]
        acse = acse_ref[rows, :]
        xdt = xs * dte
        acs_last = acse[CHUNK - 1:CHUNK, :]
        acs_t = jnp.sum(jnp.where(diag, acse, 0.0), axis=0, keepdims=True)
        decay = jnp.exp(jnp.where(causal, acse - acs_t, -jnp.inf))
        bdup = jnp.concatenate([bc, bc], axis=0)
        cb2 = lax.dot_general(cc, bdup, (((1,), (1,)), ((), ())), preferred_element_type=F32)
        xdt16 = xdt.astype(BF16)
        ys = []
        for q in range(gw // (2 * CHUNK)):
            lanes = slice(q * 2 * CHUNK, (q + 1) * 2 * CHUNK)
            wq = (cb2 * decay[:, lanes]).astype(BF16)
            xq = xdt16[:, lanes]
            bd = jnp.where(blockdiag, jnp.concatenate([xq, xq], axis=0), jnp.zeros((), BF16))
            ys.append(_dot(wq, bd))
        y = jnp.concatenate(ys, axis=1) if len(ys) > 1 else ys[0]
        state = state_ref[...]
        y = y + _dot(cc, state.astype(BF16)) * jnp.exp(acse)
        xdtw = (xdt * jnp.exp(acs_last - acse)).astype(BF16)
        state_ref[...] = state * jnp.exp(acs_last) + lax.dot_general(
            bc, xdtw, (((0,), (0,)), ((), ())), preferred_element_type=F32)
        y = y + dskip * xs
        zc = z_ref[rows, :].astype(F32)
        hh = y * (zc * jax.nn.sigmoid(zc))
        ms = jnp.mean(hh * hh, axis=-1, keepdims=True)
        o_ref[rows, :] = (hh * lax.rsqrt(ms + RMS_EPS) * nw).astype(o_ref.dtype)
        return carry

    lax.fori_loop(0, nchunk, chunk, 0)


def _ssd_scan(zx, xbc, dt_raw, dt_bias, a_log, d_skip, norm_w, batch, seq, d_inner, ts=512):
    t = zx.shape[0]
    nheads = dt_raw.shape[1]
    gw = d_inner // SSD_GROUPS
    hpg = nheads // SSD_GROUPS
    assert gw % (2 * CHUNK) == 0 and gw == hpg * SSD_HEADDIM
    ts = min(ts, seq)
    spb = seq // ts
    nb = SSD_STATE // 128
    head_of_lane = jnp.arange(d_inner, dtype=jnp.int32) // SSD_HEADDIM
    sel = (jnp.arange(nheads, dtype=jnp.int32)[:, None] == head_of_lane[None, :]).astype(BF16)
    dskip = jnp.repeat(d_skip.astype(F32), SSD_HEADDIM).reshape(1, d_inner)
    a = -jnp.exp(a_log.astype(F32)).reshape(1, nheads)
    kern = functools.partial(_ssd_scan_kernel, ts=ts, gw=gw)
    boff = d_inner // SSD_STATE
    return pl.pallas_call(
        kern,
        grid=(batch, SSD_GROUPS, spb),
        in_specs=[
            pl.BlockSpec((ts, gw), lambda b, g, s: (b * spb + s, g)),
            pl.BlockSpec((ts, SSD_STATE), lambda b, g, s: (b * spb + s, boff + g * nb)),
            pl.BlockSpec((ts, SSD_STATE), lambda b, g, s: (b * spb + s, boff + (SSD_GROUPS + g) * nb)),
            pl.BlockSpec((ts, gw), lambda b, g, s: (b * spb + s, g)),
            pl.BlockSpec((ts, nheads), lambda b, g, s: (b * spb + s, 0)),
            pl.BlockSpec((1, nheads), lambda b, g, s: (0, 0)),
            pl.BlockSpec((1, nheads), lambda b, g, s: (0, 0)),
            pl.BlockSpec((nheads, gw), lambda b, g, s: (0, g)),
            pl.BlockSpec((1, gw), lambda b, g, s: (0, g)),
            pl.BlockSpec((1, gw), lambda b, g, s: (0, g)),
        ],
        out_specs=pl.BlockSpec((ts, gw), lambda b, g, s: (b * spb + s, g)),
        out_shape=jax.ShapeDtypeStruct((t, d_inner), BF16),
        scratch_shapes=[pltpu.VMEM((SSD_STATE, gw), F32),
                        pltpu.VMEM((ts, gw), F32),
                        pltpu.VMEM((ts, gw), F32)],
        compiler_params=_cparams("parallel", "parallel", "arbitrary"),
        name="ssd_scan",
    )(xbc, xbc, xbc, zx, dt_raw, dt_bias.reshape(1, nheads).astype(F32), a, sel, dskip,
      norm_w.reshape(1, d_inner).astype(F32))


def _ssd_mixer(x16, in_proj, conv_w, conv_b, dt_bias, a_log, d_skip, norm_w, out_proj, batch, seq):
    d_inner = out_proj.shape[0]
    cdim = conv_w.shape[1]
    w16 = in_proj.astype(BF16)
    zx = _matmul(x16, w16[:, :d_inner + cdim], BF16, 1024, 1024)
    dt_raw = _matmul(x16, w16[:, d_inner + cdim:], F32, 1024, 128)
    xbc = _ssd_conv(zx, conv_w, conv_b, batch, seq, d_inner)
    h = _ssd_scan(zx, xbc, dt_raw, dt_bias, a_log, d_skip, norm_w, batch, seq, d_inner)
    return _matmul(h, out_proj.astype(BF16), F32, 1024, 512)


def _sc_gate_kernel(gb_ref, gc_ref, h_ref, w_ref, o_ref):
    u = gc_ref[...].astype(F32) * h_ref[...].astype(F32)
    w = w_ref[...]
    kw = w.shape[0]
    acc = u * w[kw - 1:kw, :]
    for k in range(kw - 1):
        acc = acc + _shift_rows(u, kw - 1 - k) * w[k:k + 1, :]
    o_ref[...] = (gb_ref[...].astype(F32) * acc).astype(o_ref.dtype)


def _sc_gate(bch, conv_w, batch, seq, d, tc=128):
    t = bch.shape[0]
    nj = d // tc
    return pl.pallas_call(
        _sc_gate_kernel,
        grid=(batch, nj),
        in_specs=[pl.BlockSpec((seq, tc), lambda b, j: (b, j)),
                  pl.BlockSpec((seq, tc), lambda b, j: (b, nj + j)),
                  pl.BlockSpec((seq, tc), lambda b, j: (b, 2 * nj + j)),
                  pl.BlockSpec((conv_w.shape[0], tc), lambda b, j: (0, j))],
        out_specs=pl.BlockSpec((seq, tc), lambda b, j: (b, j)),
        out_shape=jax.ShapeDtypeStruct((t, d), BF16),
        compiler_params=_cparams("parallel", "parallel"),
        name="shortconv_gate",
    )(bch, bch, bch, conv_w.astype(F32))


def _shortconv_mixer(x16, in_proj, conv_w, out_proj, batch, seq):
    d = x16.shape[1]
    bch = _matmul(x16, in_proj.astype(BF16), BF16, 1024, 1024)
    y = _sc_gate(bch, conv_w, batch, seq, d)
    return _matmul(y, out_proj.astype(BF16), F32, 1024, 1024)


def _top_values(s, k, rows):
    vals = []
    cur = s
    for _ in range(k):
        m = jnp.max(cur, axis=0, keepdims=True)
        vals.append(m)
        cur = jnp.where(cur == m, -jnp.inf, cur)
    if rows > k:
        vals.append(jnp.full((rows - k, s.shape[1]), -jnp.inf, F32))
    return jnp.concatenate(vals, axis=0)


def _peer_topk_kernel(qt_ref, keys_ref, s1c_ref, s2_ref, tauc_ref):
    dh = PEER_HALF_DIM
    k = PEER_TOPK

    def head(h, carry):
        r0 = pl.multiple_of(h * 2 * dh, 2 * dh)
        s1 = _dot(keys_ref[h, 0], qt_ref[pl.ds(r0, dh), :]) * LOG2E
        s2 = _dot(keys_ref[h, 1], qt_ref[pl.ds(r0 + dh, dh), :]) * LOG2E
        sa = _top_values(s1, k + 1, 24)
        sb = _top_values(s2, k + 1, 24)
        pieces = [sa[0:1, :] + sb]
        for r in range(1, 8):
            pieces.append(sa[r:r + 1, :] + sb[0:8, :])
        pieces.append(sa[8:24, :] + sb[0:1, :])
        top = _top_values(jnp.concatenate(pieces, axis=0), k + 1, k + 1)
        mx = top[0:1, :]
        z = jnp.sum(jnp.exp2(top[0:k, :] - mx), axis=0, keepdims=True)
        cst = mx + jnp.log(z) * LOG2E
        tau = 0.5 * (top[k - 1:k, :] + top[k:k + 1, :])
        s1c_ref[h] = s1 - cst
        s2_ref[h] = s2
        tauc_ref[h] = tau - cst
        return carry

    lax.fori_loop(0, PEER_HEADS, head, 0)


def _peer_topk(qt, keys16, tm=256):
    t = qt.shape[1]
    tm = _tile(t, tm)
    sc_shape = jax.ShapeDtypeStruct((PEER_HEADS, PEER_NKEYS, t), F32)
    row_shape = jax.ShapeDtypeStruct((PEER_HEADS, 1, t), F32)
    sc_spec = pl.BlockSpec((PEER_HEADS, PEER_NKEYS, tm), lambda i: (0, 0, i))
    row_spec = pl.BlockSpec((PEER_HEADS, 1, tm), lambda i: (0, 0, i))
    return pl.pallas_call(
        _peer_topk_kernel,
        grid=(t // tm,),
        in_specs=[pl.BlockSpec((qt.shape[0], tm), lambda i: (0, i)),
                  pl.BlockSpec(keys16.shape, lambda i: (0, 0, 0, 0))],
        out_specs=[sc_spec, sc_spec, row_spec],
        out_shape=[sc_shape, sc_shape, row_shape],
        compiler_params=_cparams("parallel"),
        name="peer_topk",
    )(qt, keys16)


def _peer_dense_kernel(xt_ref, u_ref, vt_ref, s1c_ref, s2_ref, tauc_ref, o_ref,
                       s1b_ref, taub_ref, act_ref, ha_ref, hb_ref, *, te, ne):
    g = pl.program_id(0)
    p = jnp.maximum(g - 1, 0)
    e = p % ne
    nk = PEER_NKEYS
    ni = te // nk
    d, tm = o_ref.shape
    rb = SUBLANES
    db = _tile(d, 1024)

    @pl.when(g == 0)
    def _():
        hb_ref[...] = jnp.zeros_like(hb_ref)

    @pl.when(e == 0)
    def _():
        o_ref[...] = jnp.zeros_like(o_ref)
        for h in range(PEER_HEADS):
            taub_ref[h] = jnp.broadcast_to(tauc_ref[h], (rb, tm))

    def step(h_new_ref, h_old_ref):
        h_new_ref[...] = _dot(u_ref[...], xt_ref[...])
        for h in range(PEER_HEADS):
            for il in range(ni):
                s1b_ref[h, il] = jnp.broadcast_to(s1c_ref[h, pl.ds(e * ni + il, 1), :], (rb, tm))
        for rc in range(te // rb):
            r0 = rc * rb
            il, j0 = r0 // nk, r0 % nk
            gate = None
            for h in range(PEER_HEADS):
                a = s1b_ref[h, il] + s2_ref[h, j0:j0 + rb, :]
                gh = jnp.where(a >= taub_ref[h], jnp.exp2(a), 0.0)
                gate = gh if gate is None else gate + gh
            pre = h_old_ref[r0:r0 + rb, :]
            act = 0.5 * pre * (1.0 + lax.erf(pre * (1.0 / math.sqrt(2.0))))
            act_ref[r0:r0 + rb, :] = (act * gate).astype(BF16)
        for r in range(d // db):
            o_ref[r * db:(r + 1) * db, :] += _dot(vt_ref[r * db:(r + 1) * db, :], act_ref[...])

    @pl.when(g % 2 == 0)
    def _():
        step(ha_ref, hb_ref)

    @pl.when(g % 2 == 1)
    def _():
        step(hb_ref, ha_ref)


def _peer_dense(xt, u16, vt16, s1c, s2, tauc, tm=512, te=1024):
    d, t = xt.shape
    nexp = u16.shape[0]
    tm, te = _tile(t, tm), _tile(nexp, te)
    assert te % PEER_NKEYS == 0
    ne = nexp // te
    npairs = (t // tm) * ne
    once = pl.Buffered(1)
    new_tile = lambda g: jnp.minimum(g, npairs - 1) // ne
    old_tile = lambda g: jnp.maximum(g - 1, 0) // ne
    sc_spec = pl.BlockSpec((PEER_HEADS, PEER_NKEYS, tm), lambda g: (0, 0, old_tile(g)), pipeline_mode=once)
    row_spec = pl.BlockSpec((PEER_HEADS, 1, tm), lambda g: (0, 0, old_tile(g)), pipeline_mode=once)
    return pl.pallas_call(
        functools.partial(_peer_dense_kernel, te=te, ne=ne),
        grid=(npairs + 1,),
        in_specs=[pl.BlockSpec((d, tm), lambda g: (0, new_tile(g)), pipeline_mode=once),
                  pl.BlockSpec((te, d), lambda g: (jnp.minimum(g, npairs - 1) % ne, 0)),
                  pl.BlockSpec((d, te), lambda g: (0, jnp.maximum(g - 1, 0) % ne)),
                  sc_spec, sc_spec, row_spec],
        out_specs=pl.BlockSpec((d, tm), lambda g: (0, old_tile(g)), pipeline_mode=once),
        out_shape=jax.ShapeDtypeStruct((d, t), F32),
        scratch_shapes=[pltpu.VMEM((PEER_HEADS, te // PEER_NKEYS, SUBLANES, tm), F32),
                        pltpu.VMEM((PEER_HEADS, SUBLANES, tm), F32),
                        pltpu.VMEM((te, tm), BF16),
                        pltpu.VMEM((te, tm), F32),
                        pltpu.VMEM((te, tm), F32)],
        compiler_params=_cparams("arbitrary"),
        name="peer_dense",
    )(xt, u16, vt16, s1c, s2, tauc)


def _peer_ffn(xt, wq, subkeys, u, v):
    qt = _matmul(wq.T.astype(BF16), xt, BF16, 1024, 1024)
    s1c, s2, tauc = _peer_topk(qt, subkeys.astype(BF16))
    return _peer_dense(xt, u.astype(BF16), v.T.astype(BF16), s1c, s2, tauc)


def kernel(x, ssd_in_proj, ssd_conv_w, ssd_conv_b, ssd_dt_bias, ssd_A_log, ssd_D, ssd_norm_w,
           ssd_out_proj, sc_in_proj, sc_conv_w, sc_out_proj, peer_wq, peer_subkeys, peer_u,
           peer_v, ln1_g, ln1_b, ln2_g, ln2_b):
    batch, seq, d = x.shape
    depth = peer_wq.shape[0]
    alpha = (2 * depth) ** 0.25
    x32 = x.reshape(batch * seq, d).astype(F32)
    x16 = x32.astype(BF16)
    for i in range(depth):
        j = i // 2
        if i % 2 == 0:
            mix = _ssd_mixer(x16, ssd_in_proj[j], ssd_conv_w[j], ssd_conv_b[j], ssd_dt_bias[j],
                             ssd_A_log[j], ssd_D[j], ssd_norm_w[j], ssd_out_proj[j], batch, seq)
        else:
            mix = _shortconv_mixer(x16, sc_in_proj[j], sc_conv_w[j], sc_out_proj[j], batch, seq)
        x32, xt16 = _residual_ln(x32, mix, ln1_g[i], ln1_b[i], alpha, transposed=False)
        ffn_t = _peer_ffn(xt16, peer_wq[i], peer_subkeys[i], peer_u[i], peer_v[i])
        x32, x16 = _residual_ln(x32, ffn_t, ln2_g[i], ln2_b[i], alpha, transposed=True)
    return x32.reshape(batch, seq, d).astype(x.dtype)
```

```python
import functools
import math

import jax
import jax.numpy as jnp
from jax import lax
from jax.experimental import pallas as pl
from jax.experimental.pallas import tpu as pltpu

F32 = jnp.float32
BF16 = jnp.bfloat16

CHUNK = 64
SSD_HEADDIM = 64
SSD_GROUPS = 8
SSD_STATE = 128
PEER_HEADS = 8
PEER_NKEYS = 128
PEER_TOPK = 16
PEER_HALF_DIM = 128
LN_EPS = 1e-5
RMS_EPS = 1e-5
LOG2E = 1.4426950408889634

SUBLANES = 8
VMEM_LIMIT_BYTES = 56 * 1024 * 1024


def _cparams(*sem):
    return pltpu.CompilerParams(dimension_semantics=sem, vmem_limit_bytes=VMEM_LIMIT_BYTES)


def _tile(n, pref):
    t = min(pref, n)
    while n % t:
        t //= 2
    return t


def _split2(x):
    hi = x.astype(BF16)
    lo = (x - hi.astype(F32)).astype(BF16)
    return hi, lo


def _dot(a, b):
    return jnp.dot(a, b, preferred_element_type=F32)


def _dot2(x, sel):
    hi, lo = _split2(x)
    return _dot(hi, sel) + _dot(lo, sel)


def _mm_kernel(a_ref, w_ref, o_ref):
    o_ref[...] = _dot(a_ref[...], w_ref[...]).astype(o_ref.dtype)


def _matmul(a, w, out_dtype, tm, tn, col0=0, n=None):
    m, k = a.shape
    n = w.shape[1] if n is None else n
    tm, tn = _tile(m, tm), _tile(n, tn)
    assert col0 % tn == 0
    joff = col0 // tn
    return pl.pallas_call(
        _mm_kernel,
        grid=(m // tm, n // tn),
        in_specs=[pl.BlockSpec((tm, k), lambda i, j: (i, 0)),
                  pl.BlockSpec((k, tn), lambda i, j: (0, joff + j))],
        out_specs=pl.BlockSpec((tm, tn), lambda i, j: (i, j)),
        out_shape=jax.ShapeDtypeStruct((m, n), out_dtype),
        compiler_params=_cparams("parallel", "arbitrary"),
        name="matmul",
    )(a, w)


def _ln_kernel(x_ref, y_ref, g_ref, b_ref, o32_ref, o16_ref, *, alpha, transposed):
    y = y_ref[...].astype(F32)
    if transposed:
        y = y.T
    h = alpha * x_ref[...] + y
    mu = jnp.mean(h, axis=-1, keepdims=True)
    d = h - mu
    var = jnp.mean(d * d, axis=-1, keepdims=True)
    out = d * lax.rsqrt(var + LN_EPS) * g_ref[...] + b_ref[...]
    o32_ref[...] = out
    if transposed:
        o16_ref[...] = out.astype(BF16)
    else:
        o16_ref[...] = out.T.astype(BF16)


def _residual_ln(x, y, g, b, alpha, transposed, tm=256):
    t, d = x.shape
    tm = _tile(t, tm)
    row = pl.BlockSpec((tm, d), lambda i: (i, 0))
    col = pl.BlockSpec((d, tm), lambda i: (0, i))
    vec = pl.BlockSpec((1, d), lambda i: (0, 0))
    return pl.pallas_call(
        functools.partial(_ln_kernel, alpha=alpha, transposed=transposed),
        grid=(t // tm,),
        in_specs=[row, col if transposed else row, vec, vec],
        out_specs=[row, row if transposed else col],
        out_shape=[jax.ShapeDtypeStruct((t, d), F32),
                   jax.ShapeDtypeStruct((t, d) if transposed else (d, t), BF16)],
        compiler_params=_cparams("parallel"),
        name="residual_ln",
    )(x, y, g.reshape(1, d).astype(F32), b.reshape(1, d).astype(F32))


def _shift_rows(h, k):
    rolled = pltpu.roll(h, k, 0)
    rows = lax.broadcasted_iota(jnp.int32, h.shape, 0)
    return jnp.where(rows >= k, rolled, 0.0)


def _ssd_conv_kernel(x_ref, w_ref, b_ref, o_ref):
    h = x_ref[...].astype(F32)
    w = w_ref[...]
    kw = w.shape[0]
    acc = h * w[kw - 1:kw, :] + b_ref[...]
    for k in range(kw - 1):
        acc = acc + _shift_rows(h, kw - 1 - k) * w[k:k + 1, :]
    o_ref[...] = (acc * jax.nn.sigmoid(acc)).astype(o_ref.dtype)


def _ssd_conv(zx, conv_w, conv_b, batch, seq, d_inner, tc=128):
    t = zx.shape[0]
    cdim = conv_w.shape[1]
    off = d_inner // tc
    return pl.pallas_call(
        _ssd_conv_kernel,
        grid=(batch, cdim // tc),
        in_specs=[pl.BlockSpec((seq, tc), lambda b, j: (b, off + j)),
                  pl.BlockSpec((conv_w.shape[0], tc), lambda b, j: (0, j)),
                  pl.BlockSpec((1, tc), lambda b, j: (0, j))],
        out_specs=pl.BlockSpec((seq, tc), lambda b, j: (b, j)),
        out_shape=jax.ShapeDtypeStruct((t, cdim), BF16),
        compiler_params=_cparams("parallel", "parallel"),
        name="ssd_conv_silu",
    )(zx, conv_w.astype(F32), conv_b.reshape(1, cdim).astype(F32))


def _ssd_scan_kernel(xs_ref, b_ref, c_ref, z_ref, dtraw_ref, dtb_ref, a_ref, sel_ref,
                     dskip_ref, nw_ref, o_ref, state_ref, dte_ref, acse_ref, *, ts, gw):
    nchunk = ts // CHUNK

    @pl.when(pl.program_id(2) == 0)
    def _():
        state_ref[...] = jnp.zeros_like(state_ref)

    xr = dtraw_ref[...] + dtb_ref[...]
    dt = jnp.maximum(xr, 0.0) + jnp.log1p(jnp.exp(-jnp.abs(xr)))
    da = dt * a_ref[...]
    ri = lax.broadcasted_iota(jnp.int32, (ts, ts), 0)
    ci = lax.broadcasted_iota(jnp.int32, (ts, ts), 1)
    tril = ((ri >= ci) & (ri // CHUNK == ci // CHUNK)).astype(BF16)
    h1, h2 = _split2(da)
    acs = _dot(tril, h1) + _dot(tril, h2)
    sel = sel_ref[...]
    dte_ref[...] = _dot2(dt, sel)
    acse_ref[...] = _dot2(acs, sel)

    li = lax.broadcasted_iota(jnp.int32, (CHUNK, gw), 0)
    si = lax.broadcasted_iota(jnp.int32, (CHUNK, gw), 1) % CHUNK
    diag = li == si
    causal = li >= si
    r2 = lax.broadcasted_iota(jnp.int32, (2 * CHUNK, 2 * CHUNK), 0) // CHUNK
    c2 = lax.broadcasted_iota(jnp.int32, (2 * CHUNK, 2 * CHUNK), 1) // CHUNK
    blockdiag = r2 == c2
    dskip = dskip_ref[...]
    nw = nw_ref[...]

    def chunk(c, carry):
        rows = pl.ds(pl.multiple_of(c * CHUNK, CHUNK), CHUNK)
        cc = c_ref[rows, :]
        bc = b_ref[rows, :]
        xs = xs_ref[rows, :].astype(F32)
        dte = dte_ref[rows, :]
        acse = acse_ref[rows, :]
        xdt = xs * dte
        acs_last = acse[CHUNK - 1:CHUNK, :]
        acs_t = jnp.sum(jnp.where(diag, acse, 0.0), axis=0, keepdims=True)
        decay = jnp.exp2(jnp.where(causal, acse - acs_t, -jnp.inf))
        bdup = jnp.concatenate([bc, bc], axis=0)
        cb2 = lax.dot_general(cc, bdup, (((1,), (1,)), ((), ())), preferred_element_type=F32)
        xdt16 = xdt.astype(BF16)
        ys = []
        for q in range(gw // (2 * CHUNK)):
            lanes = slice(q * 2 * CHUNK, (q + 1) * 2 * CHUNK)
            wq = (cb2 * decay[:, lanes]).astype(BF16)
            xq = xdt16[:, lanes]
            bd = jnp.where(blockdiag, jnp.concatenate([xq, xq], axis=0), jnp.zeros((), BF16))
            ys.append(_dot(wq, bd))
        y = jnp.concatenate(ys, axis=1) if len(ys) > 1 else ys[0]
        state = state_ref[...]
        y = y + _dot(cc, state.astype(BF16)) * jnp.exp2(acse)
        xdtw = (xdt * jnp.exp2(acs_last - acse)).astype(BF16)
        state_ref[...] = state * jnp.exp2(acs_last) + lax.dot_general(
            bc, xdtw, (((0,), (0,)), ((), ())), preferred_element_type=F32)
        y = y + dskip * xs
        zc = z_ref[rows, :].astype(F32)
        hh = y * (zc * jax.nn.sigmoid(zc))
        ms = jnp.mean(hh * hh, axis=-1, keepdims=True)
        o_ref[rows, :] = (hh * lax.rsqrt(ms + RMS_EPS) * nw).astype(o_ref.dtype)
        return carry

    lax.fori_loop(0, nchunk, chunk, 0, unroll=2)


def _ssd_scan(zx, xbc, dt_raw, dt_bias, a_log, d_skip, norm_w, batch, seq, d_inner, ts=512):
    t = zx.shape[0]
    nheads = dt_raw.shape[1]
    gw = d_inner // SSD_GROUPS
    hpg = nheads // SSD_GROUPS
    assert gw % (2 * CHUNK) == 0 and gw == hpg * SSD_HEADDIM
    ts = min(ts, seq)
    spb = seq // ts
    nb = SSD_STATE // 128
    head_of_lane = jnp.arange(d_inner, dtype=jnp.int32) // SSD_HEADDIM
    sel = (jnp.arange(nheads, dtype=jnp.int32)[:, None] == head_of_lane[None, :]).astype(BF16)
    dskip = jnp.repeat(d_skip.astype(F32), SSD_HEADDIM).reshape(1, d_inner)
    a = -jnp.exp(a_log.astype(F32)).reshape(1, nheads) * LOG2E
    kern = functools.partial(_ssd_scan_kernel, ts=ts, gw=gw)
    boff = d_inner // SSD_STATE
    return pl.pallas_call(
        kern,
        grid=(batch, SSD_GROUPS, spb),
        in_specs=[
            pl.BlockSpec((ts, gw), lambda b, g, s: (b * spb + s, g)),
            pl.BlockSpec((ts, SSD_STATE), lambda b, g, s: (b * spb + s, boff + g * nb)),
            pl.BlockSpec((ts, SSD_STATE), lambda b, g, s: (b * spb + s, boff + (SSD_GROUPS + g) * nb)),
            pl.BlockSpec((ts, gw), lambda b, g, s: (b * spb + s, g)),
            pl.BlockSpec((ts, nheads), lambda b, g, s: (b * spb + s, 0)),
            pl.BlockSpec((1, nheads), lambda b, g, s: (0, 0)),
            pl.BlockSpec((1, nheads), lambda b, g, s: (0, 0)),
            pl.BlockSpec((nheads, gw), lambda b, g, s: (0, g)),
            pl.BlockSpec((1, gw), lambda b, g, s: (0, g)),
            pl.BlockSpec((1, gw), lambda b, g, s: (0, g)),
        ],
        out_specs=pl.BlockSpec((ts, gw), lambda b, g, s: (b * spb + s, g)),
        out_shape=jax.ShapeDtypeStruct((t, d_inner), BF16),
        scratch_shapes=[pltpu.VMEM((SSD_STATE, gw), F32),
                        pltpu.VMEM((ts, gw), F32),
                        pltpu.VMEM((ts, gw), F32)],
        compiler_params=_cparams("parallel", "parallel", "arbitrary"),
        name="ssd_scan",
    )(xbc, xbc, xbc, zx, dt_raw, dt_bias.reshape(1, nheads).astype(F32), a, sel, dskip,
      norm_w.reshape(1, d_inner).astype(F32))


def _ssd_mixer(x16, in_proj, conv_w, conv_b, dt_bias, a_log, d_skip, norm_w, out_proj, batch, seq):
    d_inner = out_proj.shape[0]
    cdim = conv_w.shape[1]
    w16 = in_proj.astype(BF16)
    nheads = w16.shape[1] - d_inner - cdim
    zx = _matmul(x16, w16, BF16, 1024, 1024, n=d_inner + cdim)
    dt_raw = _matmul(x16, w16, F32, 1024, 128, col0=d_inner + cdim, n=nheads)
    xbc = _ssd_conv(zx, conv_w, conv_b, batch, seq, d_inner)
    h = _ssd_scan(zx, xbc, dt_raw, dt_bias, a_log, d_skip, norm_w, batch, seq, d_inner)
    return _matmul(h, out_proj.astype(BF16), F32, 1024, 512)


def _sc_gate_kernel(gb_ref, gc_ref, h_ref, w_ref, o_ref):
    u = gc_ref[...].astype(F32) * h_ref[...].astype(F32)
    w = w_ref[...]
    kw = w.shape[0]
    acc = u * w[kw - 1:kw, :]
    for k in range(kw - 1):
        acc = acc + _shift_rows(u, kw - 1 - k) * w[k:k + 1, :]
    o_ref[...] = (gb_ref[...].astype(F32) * acc).astype(o_ref.dtype)


def _sc_gate(bch, conv_w, batch, seq, d, tc=128):
    t = bch.shape[0]
    nj = d // tc
    return pl.pallas_call(
        _sc_gate_kernel,
        grid=(batch, nj),
        in_specs=[pl.BlockSpec((seq, tc), lambda b, j: (b, j)),
                  pl.BlockSpec((seq, tc), lambda b, j: (b, nj + j)),
                  pl.BlockSpec((seq, tc), lambda b, j: (b, 2 * nj + j)),
                  pl.BlockSpec((conv_w.shape[0], tc), lambda b, j: (0, j))],
        out_specs=pl.BlockSpec((seq, tc), lambda b, j: (b, j)),
        out_shape=jax.ShapeDtypeStruct((t, d), BF16),
        compiler_params=_cparams("parallel", "parallel"),
        name="shortconv_gate",
    )(bch, bch, bch, conv_w.astype(F32))


def _shortconv_mixer(x16, in_proj, conv_w, out_proj, batch, seq):
    d = x16.shape[1]
    bch = _matmul(x16, in_proj.astype(BF16), BF16, 1024, 1024)
    y = _sc_gate(bch, conv_w, batch, seq, d)
    return _matmul(y, out_proj.astype(BF16), F32, 1024, 1024)


def _top_values(s, k, rows):
    vals = []
    cur = s
    for _ in range(k):
        m = jnp.max(cur, axis=0, keepdims=True)
        vals.append(m)
        cur = jnp.where(cur == m, -jnp.inf, cur)
    if rows > k:
        vals.append(jnp.full((rows - k, s.shape[1]), -jnp.inf, F32))
    return jnp.concatenate(vals, axis=0)


def _peer_topk_kernel(qt_ref, keys_ref, s1c_ref, s2_ref, tauc_ref):
    dh = PEER_HALF_DIM
    k = PEER_TOPK

    def head(h, carry):
        r0 = pl.multiple_of(h * 2 * dh, 2 * dh)
        s1 = _dot(keys_ref[h, 0], qt_ref[pl.ds(r0, dh), :]) * LOG2E
        s2 = _dot(keys_ref[h, 1], qt_ref[pl.ds(r0 + dh, dh), :]) * LOG2E
        sa = _top_values(s1, k + 1, 24)
        sb = _top_values(s2, k + 1, 24)
        pieces = [sa[0:1, :] + sb]
        for r in range(1, 8):
            pieces.append(sa[r:r + 1, :] + sb[0:8, :])
        pieces.append(sa[8:24, :] + sb[0:1, :])
        top = _top_values(jnp.concatenate(pieces, axis=0), k + 1, k + 1)
        mx = top[0:1, :]
        z = jnp.sum(jnp.exp2(top[0:k, :] - mx), axis=0, keepdims=True)
        cst = mx + jnp.log(z) * LOG2E
        tau = 0.5 * (top[k - 1:k, :] + top[k:k + 1, :])
        s1c_ref[h] = s1 - cst
        s2_ref[h] = s2
        tauc_ref[h] = tau - cst
        return carry

    lax.fori_loop(0, PEER_HEADS, head, 0)


def _peer_topk(qt, keys16, tm=256):
    t = qt.shape[1]
    tm = _tile(t, tm)
    sc_shape = jax.ShapeDtypeStruct((PEER_HEADS, PEER_NKEYS, t), F32)
    row_shape = jax.ShapeDtypeStruct((PEER_HEADS, 1, t), F32)
    sc_spec = pl.BlockSpec((PEER_HEADS, PEER_NKEYS, tm), lambda i: (0, 0, i))
    row_spec = pl.BlockSpec((PEER_HEADS, 1, tm), lambda i: (0, 0, i))
    return pl.pallas_call(
        _peer_topk_kernel,
        grid=(t // tm,),
        in_specs=[pl.BlockSpec((qt.shape[0], tm), lambda i: (0, i)),
                  pl.BlockSpec(keys16.shape, lambda i: (0, 0, 0, 0))],
        out_specs=[sc_spec, sc_spec, row_spec],
        out_shape=[sc_shape, sc_shape, row_shape],
        compiler_params=_cparams("parallel"),
        name="peer_topk",
    )(qt, keys16)


def _peer_dense_kernel(xt_ref, u_ref, vt_ref, s1c_ref, s2_ref, tauc_ref, o_ref,
                       s1b_ref, taub_ref, act_ref, ha_ref, hb_ref, *, te, ne):
    g = pl.program_id(0)
    p = jnp.maximum(g - 1, 0)
    e = p % ne
    nk = PEER_NKEYS
    ni = te // nk
    d, tm = o_ref.shape
    rb = SUBLANES
    db = _tile(d, 1024)

    @pl.when(g == 0)
    def _():
        hb_ref[...] = jnp.zeros_like(hb_ref)

    @pl.when(e == 0)
    def _():
        o_ref[...] = jnp.zeros_like(o_ref)
        for h in range(PEER_HEADS):
            taub_ref[h] = jnp.broadcast_to(tauc_ref[h], (rb, tm))

    def step(h_new_ref, h_old_ref):
        h_new_ref[...] = _dot(u_ref[...], xt_ref[...])
        for h in range(PEER_HEADS):
            for il in range(ni):
                s1b_ref[h, il] = jnp.broadcast_to(s1c_ref[h, pl.ds(e * ni + il, 1), :], (rb, tm))
        for rc in range(te // rb):
            r0 = rc * rb
            il, j0 = r0 // nk, r0 % nk
            gate = None
            for h in range(PEER_HEADS):
                a = s1b_ref[h, il] + s2_ref[h, j0:j0 + rb, :]
                gh = jnp.where(a >= taub_ref[h], jnp.exp2(a), 0.0)
                gate = gh if gate is None else gate + gh
            pre = h_old_ref[r0:r0 + rb, :]
            act = 0.5 * pre * (1.0 + lax.erf(pre * (1.0 / math.sqrt(2.0))))
            act_ref[r0:r0 + rb, :] = (act * gate).astype(BF16)
        for r in range(d // db):
            o_ref[r * db:(r + 1) * db, :] += _dot(vt_ref[r * db:(r + 1) * db, :], act_ref[...])

    @pl.when(g % 2 == 0)
    def _():
        step(ha_ref, hb_ref)

    @pl.when(g % 2 == 1)
    def _():
        step(hb_ref, ha_ref)


def _peer_dense(xt, u16, vt16, s1c, s2, tauc, tm=512, te=512):
    d, t = xt.shape
    nexp = u16.shape[0]
    tm, te = _tile(t, tm), _tile(nexp, te)
    assert te % PEER_NKEYS == 0
    ne = nexp // te
    npairs = (t // tm) * ne
    new_tile = lambda g: jnp.minimum(g, npairs - 1) // ne
    old_tile = lambda g: jnp.maximum(g - 1, 0) // ne
    sc_spec = pl.BlockSpec((PEER_HEADS, PEER_NKEYS, tm), lambda g: (0, 0, old_tile(g)))
    row_spec = pl.BlockSpec((PEER_HEADS, 1, tm), lambda g: (0, 0, old_tile(g)))
    return pl.pallas_call(
        functools.partial(_peer_dense_kernel, te=te, ne=ne),
        grid=(npairs + 1,),
        in_specs=[pl.BlockSpec((d, tm), lambda g: (0, new_tile(g))),
                  pl.BlockSpec((te, d), lambda g: (jnp.minimum(g, npairs - 1) % ne, 0)),
                  pl.BlockSpec((d, te), lambda g: (0, jnp.maximum(g - 1, 0) % ne)),
                  sc_spec, sc_spec, row_spec],
        out_specs=pl.BlockSpec((d, tm), lambda g: (0, old_tile(g)), pipeline_mode=pl.Buffered(1)),
        out_shape=jax.ShapeDtypeStruct((d, t), F32),
        scratch_shapes=[pltpu.VMEM((PEER_HEADS, te // PEER_NKEYS, SUBLANES, tm), F32),
                        pltpu.VMEM((PEER_HEADS, SUBLANES, tm), F32),
                        pltpu.VMEM((te, tm), BF16),
                        pltpu.VMEM((te, tm), F32),
                        pltpu.VMEM((te, tm), F32)],
        compiler_params=_cparams("arbitrary"),
        name="peer_dense",
    )(xt, u16, vt16, s1c, s2, tauc)


def _peer_ffn(xt, wq, subkeys, u, v):
    qt = _matmul(wq.T.astype(BF16), xt, BF16, 1024, 1024)
    s1c, s2, tauc = _peer_topk(qt, subkeys.astype(BF16))
    return _peer_dense(xt, u.astype(BF16), v.T.astype(BF16), s1c, s2, tauc)


def kernel(x, ssd_in_proj, ssd_conv_w, ssd_conv_b, ssd_dt_bias, ssd_A_log, ssd_D, ssd_norm_w,
           ssd_out_proj, sc_in_proj, sc_conv_w, sc_out_proj, peer_wq, peer_subkeys, peer_u,
           peer_v, ln1_g, ln1_b, ln2_g, ln2_b):
    batch, seq, d = x.shape
    depth = peer_wq.shape[0]
    alpha = (2 * depth) ** 0.25
    x32 = x.reshape(batch * seq, d).astype(F32)
    x16 = x32.astype(BF16)
    for i in range(depth):
        j = i // 2
        if i % 2 == 0:
            mix = _ssd_mixer(x16, ssd_in_proj[j], ssd_conv_w[j], ssd_conv_b[j], ssd_dt_bias[j],
                             ssd_A_log[j], ssd_D[j], ssd_norm_w[j], ssd_out_proj[j], batch, seq)
        else:
            mix = _shortconv_mixer(x16, sc_in_proj[j], sc_conv_w[j], sc_out_proj[j], batch, seq)
        x32, xt16 = _residual_ln(x32, mix, ln1_g[i], ln1_b[i], alpha, transposed=False)
        ffn_t = _peer_ffn(xt16, peer_wq[i], peer_subkeys[i], peer_u[i], peer_v[i])
        x32, x16 = _residual_ln(x32, ffn_t, ln2_g[i], ln2_b[i], alpha, transposed=True)
    return x32.reshape(batch, seq, d).astype(x.dtype)
```

```python
import functools
import math

import jax
import jax.numpy as jnp
from jax import lax
from jax.experimental import pallas as pl
from jax.experimental.pallas import tpu as pltpu

F32 = jnp.float32
BF16 = jnp.bfloat16

CHUNK = 64
SSD_HEADDIM = 64
SSD_GROUPS = 8
SSD_STATE = 128
PEER_HEADS = 8
PEER_NKEYS = 128
PEER_TOPK = 16
PEER_HALF_DIM = 128
LN_EPS = 1e-5
RMS_EPS = 1e-5
LOG2E = 1.4426950408889634

SUBLANES = 8
VMEM_LIMIT_BYTES = 56 * 1024 * 1024


def _cparams(*sem):
    return pltpu.CompilerParams(dimension_semantics=sem, vmem_limit_bytes=VMEM_LIMIT_BYTES)


def _tile(n, pref):
    t = min(pref, n)
    while n % t:
        t //= 2
    return t


def _split2(x):
    hi = x.astype(BF16)
    lo = (x - hi.astype(F32)).astype(BF16)
    return hi, lo


def _dot(a, b):
    return jnp.dot(a, b, preferred_element_type=F32)


def _dot2(x, sel):
    hi, lo = _split2(x)
    return _dot(hi, sel) + _dot(lo, sel)


def _mm_kernel(a_ref, w_ref, o_ref):
    o_ref[...] = _dot(a_ref[...], w_ref[...]).astype(o_ref.dtype)


def _matmul(a, w, out_dtype, tm, tn, col0=0, n=None):
    m, k = a.shape
    n = w.shape[1] if n is None else n
    tm, tn = _tile(m, tm), _tile(n, tn)
    assert col0 % tn == 0
    joff = col0 // tn
    return pl.pallas_call(
        _mm_kernel,
        grid=(m // tm, n // tn),
        in_specs=[pl.BlockSpec((tm, k), lambda i, j: (i, 0)),
                  pl.BlockSpec((k, tn), lambda i, j: (0, joff + j))],
        out_specs=pl.BlockSpec((tm, tn), lambda i, j: (i, j)),
        out_shape=jax.ShapeDtypeStruct((m, n), out_dtype),
        compiler_params=_cparams("parallel", "arbitrary"),
        name="matmul",
    )(a, w)


def _ln_kernel(x_ref, y_ref, g_ref, b_ref, o32_ref, o16_ref, *, alpha, transposed):
    y = y_ref[...].astype(F32)
    if transposed:
        y = y.T
    h = alpha * x_ref[...] + y
    mu = jnp.mean(h, axis=-1, keepdims=True)
    d = h - mu
    var = jnp.mean(d * d, axis=-1, keepdims=True)
    out = d * lax.rsqrt(var + LN_EPS) * g_ref[...] + b_ref[...]
    o32_ref[...] = out
    if transposed:
        o16_ref[...] = out.astype(BF16)
    else:
        o16_ref[...] = out.T.astype(BF16)


def _residual_ln(x, y, g, b, alpha, transposed, tm=256):
    t, d = x.shape
    tm = _tile(t, tm)
    row = pl.BlockSpec((tm, d), lambda i: (i, 0))
    col = pl.BlockSpec((d, tm), lambda i: (0, i))
    vec = pl.BlockSpec((1, d), lambda i: (0, 0))
    return pl.pallas_call(
        functools.partial(_ln_kernel, alpha=alpha, transposed=transposed),
        grid=(t // tm,),
        in_specs=[row, col if transposed else row, vec, vec],
        out_specs=[row, row if transposed else col],
        out_shape=[jax.ShapeDtypeStruct((t, d), F32),
                   jax.ShapeDtypeStruct((t, d) if transposed else (d, t), BF16)],
        compiler_params=_cparams("parallel"),
        name="residual_ln",
    )(x, y, g.reshape(1, d).astype(F32), b.reshape(1, d).astype(F32))


def _shift_rows(h, k):
    rolled = pltpu.roll(h, k, 0)
    rows = lax.broadcasted_iota(jnp.int32, h.shape, 0)
    return jnp.where(rows >= k, rolled, 0.0)


def _ssd_conv_kernel(x_ref, w_ref, b_ref, o_ref):
    h = x_ref[...].astype(F32)
    w = w_ref[...]
    kw = w.shape[0]
    acc = h * w[kw - 1:kw, :] + b_ref[...]
    for k in range(kw - 1):
        acc = acc + _shift_rows(h, kw - 1 - k) * w[k:k + 1, :]
    o_ref[...] = (acc * jax.nn.sigmoid(acc)).astype(o_ref.dtype)


def _ssd_conv(zx, conv_w, conv_b, batch, seq, d_inner, tc=128):
    t = zx.shape[0]
    cdim = conv_w.shape[1]
    off = d_inner // tc
    return pl.pallas_call(
        _ssd_conv_kernel,
        grid=(batch, cdim // tc),
        in_specs=[pl.BlockSpec((seq, tc), lambda b, j: (b, off + j)),
                  pl.BlockSpec((conv_w.shape[0], tc), lambda b, j: (0, j)),
                  pl.BlockSpec((1, tc), lambda b, j: (0, j))],
        out_specs=pl.BlockSpec((seq, tc), lambda b, j: (b, j)),
        out_shape=jax.ShapeDtypeStruct((t, cdim), BF16),
        compiler_params=_cparams("parallel", "parallel"),
        name="ssd_conv_silu",
    )(zx, conv_w.astype(F32), conv_b.reshape(1, cdim).astype(F32))


def _ssd_scan_kernel(xs_ref, b_ref, c_ref, z_ref, dtraw_ref, dtb_ref, a_ref, sel_ref,
                     dskip_ref, nw_ref, o_ref, state_ref, dte_ref, acse_ref, *, ts, gw):
    nchunk = ts // CHUNK

    @pl.when(pl.program_id(2) == 0)
    def _():
        state_ref[...] = jnp.zeros_like(state_ref)

    xr = dtraw_ref[...] + dtb_ref[...]
    dt = jnp.maximum(xr, 0.0) + jnp.log1p(jnp.exp(-jnp.abs(xr)))
    da = dt * a_ref[...]
    ri = lax.broadcasted_iota(jnp.int32, (ts, ts), 0)
    ci = lax.broadcasted_iota(jnp.int32, (ts, ts), 1)
    tril = ((ri >= ci) & (ri // CHUNK == ci // CHUNK)).astype(BF16)
    h1, h2 = _split2(da)
    acs = _dot(tril, h1) + _dot(tril, h2)
    sel = sel_ref[...]
    dte_ref[...] = _dot2(dt, sel)
    acse_ref[...] = _dot2(acs, sel)

    li = lax.broadcasted_iota(jnp.int32, (CHUNK, gw), 0)
    si = lax.broadcasted_iota(jnp.int32, (CHUNK, gw), 1) % CHUNK
    diag = li == si
    causal = li >= si
    r2 = lax.broadcasted_iota(jnp.int32, (2 * CHUNK, 2 * CHUNK), 0) // CHUNK
    c2 = lax.broadcasted_iota(jnp.int32, (2 * CHUNK, 2 * CHUNK), 1) // CHUNK
    blockdiag = r2 == c2
    dskip = dskip_ref[...]
    nw = nw_ref[...]

    def chunk(c, carry):
        rows = pl.ds(pl.multiple_of(c * CHUNK, CHUNK), CHUNK)
        cc = c_ref[rows, :]
        bc = b_ref[rows, :]
        xs = xs_ref[rows, :].astype(F32)
        dte = dte_ref[rows, :]
        acse = acse_ref[rows, :]
        xdt = xs * dte
        acs_last = acse[CHUNK - 1:CHUNK, :]
        acs_t = jnp.sum(jnp.where(diag, acse, 0.0), axis=0, keepdims=True)
        decay = jnp.exp2(jnp.where(causal, acse - acs_t, -jnp.inf))
        bdup = jnp.concatenate([bc, bc], axis=0)
        cb2 = lax.dot_general(cc, bdup, (((1,), (1,)), ((), ())), preferred_element_type=F32)
        xdt16 = xdt.astype(BF16)
        ys = []
        for q in range(gw // (2 * CHUNK)):
            lanes = slice(q * 2 * CHUNK, (q + 1) * 2 * CHUNK)
            wq = (cb2 * decay[:, lanes]).astype(BF16)
            xq = xdt16[:, lanes]
            bd = jnp.where(blockdiag, jnp.concatenate([xq, xq], axis=0), jnp.zeros((), BF16))
            ys.append(_dot(wq, bd))
        y = jnp.concatenate(ys, axis=1) if len(ys) > 1 else ys[0]
        state = state_ref[...]
        y = y + _dot(cc, state.astype(BF16)) * jnp.exp2(acse)
        xdtw = (xdt * jnp.exp2(acs_last - acse)).astype(BF16)
        state_ref[...] = state * jnp.exp2(acs_last) + lax.dot_general(
            bc, xdtw, (((0,), (0,)), ((), ())), preferred_element_type=F32)
        y = y + dskip * xs
        zc = z_ref[rows, :].astype(F32)
        hh = y * (zc * jax.nn.sigmoid(zc))
        ms = jnp.mean(hh * hh, axis=-1, keepdims=True)
        o_ref[rows, :] = (hh * lax.rsqrt(ms + RMS_EPS) * nw).astype(o_ref.dtype)
        return carry

    lax.fori_loop(0, nchunk, chunk, 0, unroll=2)


def _ssd_scan(zx, xbc, dt_raw, dt_bias, a_log, d_skip, norm_w, batch, seq, d_inner, ts=512):
    t = zx.shape[0]
    nheads = dt_raw.shape[1]
    gw = d_inner // SSD_GROUPS
    hpg = nheads // SSD_GROUPS
    assert gw % (2 * CHUNK) == 0 and gw == hpg * SSD_HEADDIM
    ts = min(ts, seq)
    spb = seq // ts
    nb = SSD_STATE // 128
    head_of_lane = jnp.arange(d_inner, dtype=jnp.int32) // SSD_HEADDIM
    sel = (jnp.arange(nheads, dtype=jnp.int32)[:, None] == head_of_lane[None, :]).astype(BF16)
    dskip = jnp.repeat(d_skip.astype(F32), SSD_HEADDIM).reshape(1, d_inner)
    a = -jnp.exp(a_log.astype(F32)).reshape(1, nheads) * LOG2E
    kern = functools.partial(_ssd_scan_kernel, ts=ts, gw=gw)
    boff = d_inner // SSD_STATE
    return pl.pallas_call(
        kern,
        grid=(batch, SSD_GROUPS, spb),
        in_specs=[
            pl.BlockSpec((ts, gw), lambda b, g, s: (b * spb + s, g)),
            pl.BlockSpec((ts, SSD_STATE), lambda b, g, s: (b * spb + s, boff + g * nb)),
            pl.BlockSpec((ts, SSD_STATE), lambda b, g, s: (b * spb + s, boff + (SSD_GROUPS + g) * nb)),
            pl.BlockSpec((ts, gw), lambda b, g, s: (b * spb + s, g)),
            pl.BlockSpec((ts, nheads), lambda b, g, s: (b * spb + s, 0)),
            pl.BlockSpec((1, nheads), lambda b, g, s: (0, 0)),
            pl.BlockSpec((1, nheads), lambda b, g, s: (0, 0)),
            pl.BlockSpec((nheads, gw), lambda b, g, s: (0, g)),
            pl.BlockSpec((1, gw), lambda b, g, s: (0, g)),
            pl.BlockSpec((1, gw), lambda b, g, s: (0, g)),
        ],
        out_specs=pl.BlockSpec((ts, gw), lambda b, g, s: (b * spb + s, g)),
        out_shape=jax.ShapeDtypeStruct((t, d_inner), BF16),
        scratch_shapes=[pltpu.VMEM((SSD_STATE, gw), F32),
                        pltpu.VMEM((ts, gw), F32),
                        pltpu.VMEM((ts, gw), F32)],
        compiler_params=_cparams("parallel", "parallel", "arbitrary"),
        name="ssd_scan",
    )(xbc, xbc, xbc, zx, dt_raw, dt_bias.reshape(1, nheads).astype(F32), a, sel, dskip,
      norm_w.reshape(1, d_inner).astype(F32))


def _ssd_mixer(x16, in_proj, conv_w, conv_b, dt_bias, a_log, d_skip, norm_w, out_proj, batch, seq):
    d_inner = out_proj.shape[0]
    cdim = conv_w.shape[1]
    w16 = in_proj.astype(BF16)
    nheads = w16.shape[1] - d_inner - cdim
    zx = _matmul(x16, w16, BF16, 1024, 1024, n=d_inner + cdim)
    dt_raw = _matmul(x16, w16, F32, 1024, 128, col0=d_inner + cdim, n=nheads)
    xbc = _ssd_conv(zx, conv_w, conv_b, batch, seq, d_inner)
    h = _ssd_scan(zx, xbc, dt_raw, dt_bias, a_log, d_skip, norm_w, batch, seq, d_inner)
    return _matmul(h, out_proj.astype(BF16), F32, 1024, 512)


def _sc_gate_kernel(gb_ref, gc_ref, h_ref, w_ref, o_ref):
    u = gc_ref[...].astype(F32) * h_ref[...].astype(F32)
    w = w_ref[...]
    kw = w.shape[0]
    acc = u * w[kw - 1:kw, :]
    for k in range(kw - 1):
        acc = acc + _shift_rows(u, kw - 1 - k) * w[k:k + 1, :]
    o_ref[...] = (gb_ref[...].astype(F32) * acc).astype(o_ref.dtype)


def _sc_gate(bch, conv_w, batch, seq, d, tc=128):
    t = bch.shape[0]
    nj = d // tc
    return pl.pallas_call(
        _sc_gate_kernel,
        grid=(batch, nj),
        in_specs=[pl.BlockSpec((seq, tc), lambda b, j: (b, j)),
                  pl.BlockSpec((seq, tc), lambda b, j: (b, nj + j)),
                  pl.BlockSpec((seq, tc), lambda b, j: (b, 2 * nj + j)),
                  pl.BlockSpec((conv_w.shape[0], tc), lambda b, j: (0, j))],
        out_specs=pl.BlockSpec((seq, tc), lambda b, j: (b, j)),
        out_shape=jax.ShapeDtypeStruct((t, d), BF16),
        compiler_params=_cparams("parallel", "parallel"),
        name="shortconv_gate",
    )(bch, bch, bch, conv_w.astype(F32))


def _shortconv_mixer(x16, in_proj, conv_w, out_proj, batch, seq):
    d = x16.shape[1]
    bch = _matmul(x16, in_proj.astype(BF16), BF16, 1024, 1024)
    y = _sc_gate(bch, conv_w, batch, seq, d)
    return _matmul(y, out_proj.astype(BF16), F32, 1024, 1024)


def _top_values(s, k, rows):
    vals = []
    cur = s
    for _ in range(k):
        m = jnp.max(cur, axis=0, keepdims=True)
        vals.append(m)
        cur = jnp.where(cur == m, -jnp.inf, cur)
    if rows > k:
        vals.append(jnp.full((rows - k, s.shape[1]), -jnp.inf, F32))
    return jnp.concatenate(vals, axis=0)


def _peer_topk_kernel(qt_ref, keys_ref, s1c_ref, s2_ref, tauc_ref):
    dh = PEER_HALF_DIM
    k = PEER_TOPK

    def head(h, carry):
        r0 = pl.multiple_of(h * 2 * dh, 2 * dh)
        s1 = _dot(keys_ref[h, 0], qt_ref[pl.ds(r0, dh), :]) * LOG2E
        s2 = _dot(keys_ref[h, 1], qt_ref[pl.ds(r0 + dh, dh), :]) * LOG2E
        sa = _top_values(s1, k + 1, 24)
        sb = _top_values(s2, k + 1, 24)
        pieces = [sa[0:1, :] + sb]
        for r in range(1, 8):
            pieces.append(sa[r:r + 1, :] + sb[0:8, :])
        pieces.append(sa[8:24, :] + sb[0:1, :])
        top = _top_values(jnp.concatenate(pieces, axis=0), k + 1, k + 1)
        mx = top[0:1, :]
        z = jnp.sum(jnp.exp2(top[0:k, :] - mx), axis=0, keepdims=True)
        cst = mx + jnp.log(z) * LOG2E
        tau = 0.5 * (top[k - 1:k, :] + top[k:k + 1, :])
        s1c_ref[h] = s1 - cst
        s2_ref[h] = s2
        tauc_ref[h] = tau - cst
        return carry

    lax.fori_loop(0, PEER_HEADS, head, 0)


def _peer_topk(qt, keys16, tm=256):
    t = qt.shape[1]
    tm = _tile(t, tm)
    sc_shape = jax.ShapeDtypeStruct((PEER_HEADS, PEER_NKEYS, t), F32)
    row_shape = jax.ShapeDtypeStruct((PEER_HEADS, 1, t), F32)
    sc_spec = pl.BlockSpec((PEER_HEADS, PEER_NKEYS, tm), lambda i: (0, 0, i))
    row_spec = pl.BlockSpec((PEER_HEADS, 1, tm), lambda i: (0, 0, i))
    return pl.pallas_call(
        _peer_topk_kernel,
        grid=(t // tm,),
        in_specs=[pl.BlockSpec((qt.shape[0], tm), lambda i: (0, i)),
                  pl.BlockSpec(keys16.shape, lambda i: (0, 0, 0, 0))],
        out_specs=[sc_spec, sc_spec, row_spec],
        out_shape=[sc_shape, sc_shape, row_shape],
        compiler_params=_cparams("parallel"),
        name="peer_topk",
    )(qt, keys16)


def _peer_dense_kernel(xt_ref, u_ref, vt_ref, s1c_ref, s2_ref, tauc_ref, o_ref,
                       h_ref, act_ref, s1b_ref, taub_ref, *, te, ne):
    g = pl.program_id(0)
    e1 = jnp.maximum(g - 1, 0) % ne
    e2 = jnp.maximum(g - 2, 0) % ne
    new = g % 2
    old = 1 - new
    nk = PEER_NKEYS
    ni = te // nk
    d, tm = o_ref.shape
    db = d // ni
    rb = 2 * SUBLANES

    @pl.when(g == 0)
    def _():
        h_ref[...] = jnp.zeros_like(h_ref)
        act_ref[...] = jnp.zeros_like(act_ref)

    @pl.when(e2 == 0)
    def _():
        o_ref[...] = jnp.zeros_like(o_ref)

    @pl.when(e1 == 0)
    def _():
        for h in range(PEER_HEADS):
            taub_ref[h] = jnp.broadcast_to(tauc_ref[h], (SUBLANES, tm))

    h_ref[new] = _dot(u_ref[...], xt_ref[...])

    def piece(k, carry):
        rows = pl.ds(pl.multiple_of(k * db, db), db)
        o_ref[rows, :] += _dot(vt_ref[rows, :], act_ref[old])
        for h in range(PEER_HEADS):
            s1b_ref[h] = jnp.broadcast_to(s1c_ref[h, pl.ds(e1 * ni + k, 1), :], (SUBLANES, tm))
        for jb in range(nk // rb):
            halves = []
            for j0 in (jb * rb, jb * rb + SUBLANES):
                gate = None
                for h in range(PEER_HEADS):
                    a = s1b_ref[h] + s2_ref[h, j0:j0 + SUBLANES, :]
                    gh = jnp.where(a >= taub_ref[h], jnp.exp2(a), 0.0)
                    gate = gh if gate is None else gate + gh
                halves.append(gate)
            r0 = pl.multiple_of(k * nk, nk) + jb * rb
            pre = h_ref[old, pl.ds(r0, rb), :]
            act = 0.5 * pre * (1.0 + lax.erf(pre * (1.0 / math.sqrt(2.0))))
            act_ref[new, pl.ds(r0, rb), :] = (act * jnp.concatenate(halves, axis=0)).astype(BF16)
        return carry

    lax.fori_loop(0, ni, piece, 0)


def _peer_dense(xt, u16, vt16, s1c, s2, tauc, tm=512, te=512):
    d, t = xt.shape
    nexp = u16.shape[0]
    tm, te = _tile(t, tm), _tile(nexp, te)
    assert te % PEER_NKEYS == 0 and d % (te // PEER_NKEYS) == 0
    ne = nexp // te
    npairs = (t // tm) * ne
    pair = lambda g, lag: jnp.clip(g - lag, 0, npairs - 1)
    sc_spec = pl.BlockSpec((PEER_HEADS, PEER_NKEYS, tm), lambda g: (0, 0, pair(g, 1) // ne))
    row_spec = pl.BlockSpec((PEER_HEADS, 1, tm), lambda g: (0, 0, pair(g, 1) // ne))
    return pl.pallas_call(
        functools.partial(_peer_dense_kernel, te=te, ne=ne),
        grid=(npairs + 2,),
        in_specs=[pl.BlockSpec((d, tm), lambda g: (0, pair(g, 0) // ne)),
                  pl.BlockSpec((te, d), lambda g: (pair(g, 0) % ne, 0)),
                  pl.BlockSpec((d, te), lambda g: (0, pair(g, 2) % ne)),
                  sc_spec, sc_spec, row_spec],
        out_specs=pl.BlockSpec((d, tm), lambda g: (0, pair(g, 2) // ne), pipeline_mode=pl.Buffered(1)),
        out_shape=jax.ShapeDtypeStruct((d, t), F32),
        scratch_shapes=[pltpu.VMEM((2, te, tm), F32),
                        pltpu.VMEM((2, te, tm), BF16),
                        pltpu.VMEM((PEER_HEADS, SUBLANES, tm), F32),
                        pltpu.VMEM((PEER_HEADS, SUBLANES, tm), F32)],
        compiler_params=_cparams("arbitrary"),
        name="peer_dense",
    )(xt, u16, vt16, s1c, s2, tauc)


def _peer_ffn(xt, wq, subkeys, u, v):
    qt = _matmul(wq.T.astype(BF16), xt, BF16, 1024, 1024)
    s1c, s2, tauc = _peer_topk(qt, subkeys.astype(BF16))
    return _peer_dense(xt, u.astype(BF16), v.T.astype(BF16), s1c, s2, tauc)


def kernel(x, ssd_in_proj, ssd_conv_w, ssd_conv_b, ssd_dt_bias, ssd_A_log, ssd_D, ssd_norm_w,
           ssd_out_proj, sc_in_proj, sc_conv_w, sc_out_proj, peer_wq, peer_subkeys, peer_u,
           peer_v, ln1_g, ln1_b, ln2_g, ln2_b):
    batch, seq, d = x.shape
    depth = peer_wq.shape[0]
    alpha = (2 * depth) ** 0.25
    x32 = x.reshape(batch * seq, d).astype(F32)
    x16 = x32.astype(BF16)
    for i in range(depth):
        j = i // 2
        if i % 2 == 0:
            mix = _ssd_mixer(x16, ssd_in_proj[j], ssd_conv_w[j], ssd_conv_b[j], ssd_dt_bias[j],
                             ssd_A_log[j], ssd_D[j], ssd_norm_w[j], ssd_out_proj[j], batch, seq)
        else:
            mix = _shortconv_mixer(x16, sc_in_proj[j], sc_conv_w[j], sc_out_proj[j], batch, seq)
        x32, xt16 = _residual_ln(x32, mix, ln1_g[i], ln1_b[i], alpha, transposed=False)
        ffn_t = _peer_ffn(xt16, peer_wq[i], peer_subkeys[i], peer_u[i], peer_v[i])
        x32, x16 = _residual_ln(x32, ffn_t, ln2_g[i], ln2_b[i], alpha, transposed=True)
    return x32.reshape(batch, seq, d).astype(x.dtype)
```

```python
import functools
import math

import jax
import jax.numpy as jnp
from jax import lax
from jax.experimental import pallas as pl
from jax.experimental.pallas import tpu as pltpu

F32 = jnp.float32
BF16 = jnp.bfloat16

CHUNK = 64
SSD_HEADDIM = 64
SSD_GROUPS = 8
SSD_STATE = 128
PEER_HEADS = 8
PEER_NKEYS = 128
PEER_TOPK = 16
PEER_HALF_DIM = 128
LN_EPS = 1e-5
RMS_EPS = 1e-5
LOG2E = 1.4426950408889634

SUBLANES = 8
VMEM_LIMIT_BYTES = 56 * 1024 * 1024


def _cparams(*sem):
    return pltpu.CompilerParams(dimension_semantics=sem, vmem_limit_bytes=VMEM_LIMIT_BYTES)


def _tile(n, pref):
    t = min(pref, n)
    while n % t:
        t //= 2
    return t


def _split2(x):
    hi = x.astype(BF16)
    lo = (x - hi.astype(F32)).astype(BF16)
    return hi, lo


def _dot(a, b):
    return jnp.dot(a, b, preferred_element_type=F32)


def _dot2(x, sel):
    hi, lo = _split2(x)
    return _dot(hi, sel) + _dot(lo, sel)


def _mm_kernel(a_ref, w_ref, o_ref):
    o_ref[...] = _dot(a_ref[...], w_ref[...]).astype(o_ref.dtype)


def _matmul(a, w, out_dtype, tm, tn):
    m, k = a.shape
    n = w.shape[1]
    tm, tn = _tile(m, tm), _tile(n, tn)
    return pl.pallas_call(
        _mm_kernel,
        grid=(m // tm, n // tn),
        in_specs=[pl.BlockSpec((tm, k), lambda i, j: (i, 0)),
                  pl.BlockSpec((k, tn), lambda i, j: (0, j))],
        out_specs=pl.BlockSpec((tm, tn), lambda i, j: (i, j)),
        out_shape=jax.ShapeDtypeStruct((m, n), out_dtype),
        compiler_params=_cparams("parallel", "arbitrary"),
        name="matmul",
    )(a, w)


def _mm_wcast_kernel(a_ref, w_ref, o_ref, w16_ref):
    @pl.when(pl.program_id(1) == 0)
    def _():
        w16_ref[...] = w_ref[...].astype(BF16)

    o_ref[...] = _dot(a_ref[...], w16_ref[...]).astype(o_ref.dtype)


def _matmul_w32(a, w32, out_dtype, tm, tn, n=None):
    m, k = a.shape
    n = w32.shape[1] if n is None else n
    tm, tn = _tile(m, tm), _tile(n, tn)
    return pl.pallas_call(
        _mm_wcast_kernel,
        grid=(n // tn, m // tm),
        in_specs=[pl.BlockSpec((tm, k), lambda j, i: (i, 0)),
                  pl.BlockSpec((k, tn), lambda j, i: (0, j))],
        out_specs=pl.BlockSpec((tm, tn), lambda j, i: (i, j)),
        out_shape=jax.ShapeDtypeStruct((m, n), out_dtype),
        scratch_shapes=[pltpu.VMEM((k, tn), BF16)],
        compiler_params=_cparams("parallel", "arbitrary"),
        name="matmul_w32",
    )(a, w32)


def _ln_kernel(x_ref, y_ref, g_ref, b_ref, o32_ref, o16_ref, *, alpha, transposed):
    y = y_ref[...].astype(F32)
    if transposed:
        y = y.T
    h = alpha * x_ref[...] + y
    mu = jnp.mean(h, axis=-1, keepdims=True)
    d = h - mu
    var = jnp.mean(d * d, axis=-1, keepdims=True)
    out = d * lax.rsqrt(var + LN_EPS) * g_ref[...] + b_ref[...]
    o32_ref[...] = out
    if transposed:
        o16_ref[...] = out.astype(BF16)
    else:
        o16_ref[...] = out.T.astype(BF16)


def _residual_ln(x, y, g, b, alpha, transposed, tm=256):
    t, d = x.shape
    tm = _tile(t, tm)
    row = pl.BlockSpec((tm, d), lambda i: (i, 0))
    col = pl.BlockSpec((d, tm), lambda i: (0, i))
    vec = pl.BlockSpec((1, d), lambda i: (0, 0))
    return pl.pallas_call(
        functools.partial(_ln_kernel, alpha=alpha, transposed=transposed),
        grid=(t // tm,),
        in_specs=[row, col if transposed else row, vec, vec],
        out_specs=[row, row if transposed else col],
        out_shape=[jax.ShapeDtypeStruct((t, d), F32),
                   jax.ShapeDtypeStruct((t, d) if transposed else (d, t), BF16)],
        compiler_params=_cparams("parallel"),
        name="residual_ln",
    )(x, y, g.reshape(1, d).astype(F32), b.reshape(1, d).astype(F32))


def _shift_rows(h, k):
    rolled = pltpu.roll(h, k, 0)
    rows = lax.broadcasted_iota(jnp.int32, h.shape, 0)
    return jnp.where(rows >= k, rolled, 0.0)


def _ssd_conv_kernel(x_ref, w_ref, b_ref, o_ref):
    h = x_ref[...].astype(F32)
    w = w_ref[...]
    kw = w.shape[0]
    acc = h * w[kw - 1:kw, :] + b_ref[...]
    for k in range(kw - 1):
        acc = acc + _shift_rows(h, kw - 1 - k) * w[k:k + 1, :]
    o_ref[...] = (acc * jax.nn.sigmoid(acc)).astype(o_ref.dtype)


def _ssd_conv(zx, conv_w, conv_b, batch, seq, d_inner, tc=128):
    t = zx.shape[0]
    cdim = conv_w.shape[1]
    off = d_inner // tc
    return pl.pallas_call(
        _ssd_conv_kernel,
        grid=(batch, cdim // tc),
        in_specs=[pl.BlockSpec((seq, tc), lambda b, j: (b, off + j)),
                  pl.BlockSpec((conv_w.shape[0], tc), lambda b, j: (0, j)),
                  pl.BlockSpec((1, tc), lambda b, j: (0, j))],
        out_specs=pl.BlockSpec((seq, tc), lambda b, j: (b, j)),
        out_shape=jax.ShapeDtypeStruct((t, cdim), BF16),
        compiler_params=_cparams("parallel", "parallel"),
        name="ssd_conv_silu",
    )(zx, conv_w.astype(F32), conv_b.reshape(1, cdim).astype(F32))


def _ssd_scan_kernel(xs_ref, b_ref, c_ref, z_ref, dtraw_ref, dtb_ref, a_ref, sel_ref,
                     dskip_ref, nw_ref, o_ref, state_ref, dte_ref, acse_ref, *, ts, gw):
    nchunk = ts // CHUNK

    @pl.when(pl.program_id(2) == 0)
    def _():
        state_ref[...] = jnp.zeros_like(state_ref)

    xr = dtraw_ref[...] + dtb_ref[...]
    dt = jnp.maximum(xr, 0.0) + jnp.log1p(jnp.exp(-jnp.abs(xr)))
    da = dt * a_ref[...]
    ri = lax.broadcasted_iota(jnp.int32, (ts, ts), 0)
    ci = lax.broadcasted_iota(jnp.int32, (ts, ts), 1)
    tril = ((ri >= ci) & (ri // CHUNK == ci // CHUNK)).astype(BF16)
    h1, h2 = _split2(da)
    acs = _dot(tril, h1) + _dot(tril, h2)
    sel = sel_ref[...]
    dte_ref[...] = _dot2(dt, sel)
    acse_ref[...] = _dot2(acs, sel)

    li = lax.broadcasted_iota(jnp.int32, (CHUNK, gw), 0)
    si = lax.broadcasted_iota(jnp.int32, (CHUNK, gw), 1) % CHUNK
    diag = li == si
    causal = li >= si
    r2 = lax.broadcasted_iota(jnp.int32, (2 * CHUNK, 2 * CHUNK), 0) // CHUNK
    c2 = lax.broadcasted_iota(jnp.int32, (2 * CHUNK, 2 * CHUNK), 1) // CHUNK
    blockdiag = r2 == c2
    dskip = dskip_ref[...]
    nw = nw_ref[...]

    def chunk(c, carry):
        rows = pl.ds(pl.multiple_of(c * CHUNK, CHUNK), CHUNK)
        cc = c_ref[rows, :]
        bc = b_ref[rows, :]
        xs = xs_ref[rows, :].astype(F32)
        dte = dte_ref[rows, :]
        acse = acse_ref[rows, :]
        xdt = xs * dte
        acs_last = acse[CHUNK - 1:CHUNK, :]
        acs_t = jnp.sum(jnp.where(diag, acse, 0.0), axis=0, keepdims=True)
        decay = jnp.exp2(jnp.where(causal, acse - acs_t, -jnp.inf))
        bdup = jnp.concatenate([bc, bc], axis=0)
        cb2 = lax.dot_general(cc, bdup, (((1,), (1,)), ((), ())), preferred_element_type=F32)
        xdt16 = xdt.astype(BF16)
        ys = []
        for q in range(gw // (2 * CHUNK)):
            lanes = slice(q * 2 * CHUNK, (q + 1) * 2 * CHUNK)
            wq = (cb2 * decay[:, lanes]).astype(BF16)
            xq = xdt16[:, lanes]
            bd = jnp.where(blockdiag, jnp.concatenate([xq, xq], axis=0), jnp.zeros((), BF16))
            ys.append(_dot(wq, bd))
        y = jnp.concatenate(ys, axis=1) if len(ys) > 1 else ys[0]
        state = state_ref[...]
        y = y + _dot(cc, state.astype(BF16)) * jnp.exp2(acse)
        xdtw = (xdt * jnp.exp2(acs_last - acse)).astype(BF16)
        state_ref[...] = state * jnp.exp2(acs_last) + lax.dot_general(
            bc, xdtw, (((0,), (0,)), ((), ())), preferred_element_type=F32)
        y = y + dskip * xs
        zc = z_ref[rows, :].astype(F32)
        hh = y * (zc * jax.nn.sigmoid(zc))
        ms = jnp.mean(hh * hh, axis=-1, keepdims=True)
        o_ref[rows, :] = (hh * lax.rsqrt(ms + RMS_EPS) * nw).astype(o_ref.dtype)
        return carry

    lax.fori_loop(0, nchunk, chunk, 0, unroll=2)


def _ssd_scan(zx, xbc, dt_raw, dt_bias, a_log, d_skip, norm_w, batch, seq, d_inner, ts=512):
    t = zx.shape[0]
    nheads = dt_raw.shape[1]
    gw = d_inner // SSD_GROUPS
    hpg = nheads // SSD_GROUPS
    assert gw % (2 * CHUNK) == 0 and gw == hpg * SSD_HEADDIM
    ts = min(ts, seq)
    spb = seq // ts
    nb = SSD_STATE // 128
    head_of_lane = jnp.arange(d_inner, dtype=jnp.int32) // SSD_HEADDIM
    sel = (jnp.arange(nheads, dtype=jnp.int32)[:, None] == head_of_lane[None, :]).astype(BF16)
    dskip = jnp.repeat(d_skip.astype(F32), SSD_HEADDIM).reshape(1, d_inner)
    a = -jnp.exp(a_log.astype(F32)).reshape(1, nheads) * LOG2E
    kern = functools.partial(_ssd_scan_kernel, ts=ts, gw=gw)
    boff = d_inner // SSD_STATE
    return pl.pallas_call(
        kern,
        grid=(batch, SSD_GROUPS, spb),
        in_specs=[
            pl.BlockSpec((ts, gw), lambda b, g, s: (b * spb + s, g)),
            pl.BlockSpec((ts, SSD_STATE), lambda b, g, s: (b * spb + s, boff + g * nb)),
            pl.BlockSpec((ts, SSD_STATE), lambda b, g, s: (b * spb + s, boff + (SSD_GROUPS + g) * nb)),
            pl.BlockSpec((ts, gw), lambda b, g, s: (b * spb + s, g)),
            pl.BlockSpec((ts, nheads), lambda b, g, s: (b * spb + s, 0)),
            pl.BlockSpec((1, nheads), lambda b, g, s: (0, 0)),
            pl.BlockSpec((1, nheads), lambda b, g, s: (0, 0)),
            pl.BlockSpec((nheads, gw), lambda b, g, s: (0, g)),
            pl.BlockSpec((1, gw), lambda b, g, s: (0, g)),
            pl.BlockSpec((1, gw), lambda b, g, s: (0, g)),
        ],
        out_specs=pl.BlockSpec((ts, gw), lambda b, g, s: (b * spb + s, g)),
        out_shape=jax.ShapeDtypeStruct((t, d_inner), BF16),
        scratch_shapes=[pltpu.VMEM((SSD_STATE, gw), F32),
                        pltpu.VMEM((ts, gw), F32),
                        pltpu.VMEM((ts, gw), F32)],
        compiler_params=_cparams("parallel", "parallel", "arbitrary"),
        name="ssd_scan",
    )(xbc, xbc, xbc, zx, dt_raw, dt_bias.reshape(1, nheads).astype(F32), a, sel, dskip,
      norm_w.reshape(1, d_inner).astype(F32))


def _ssd_mixer(x16, in_proj, conv_w, conv_b, dt_bias, a_log, d_skip, norm_w, out_proj, batch, seq):
    d_inner = out_proj.shape[0]
    cdim = conv_w.shape[1]
    zx = _matmul_w32(x16, in_proj, BF16, 1024, 512, n=d_inner + cdim)
    dt_raw = _matmul(x16, in_proj[:, d_inner + cdim:].astype(BF16), F32, 1024, 128)
    xbc = _ssd_conv(zx, conv_w, conv_b, batch, seq, d_inner)
    h = _ssd_scan(zx, xbc, dt_raw, dt_bias, a_log, d_skip, norm_w, batch, seq, d_inner)
    return _matmul(h, out_proj.astype(BF16), BF16, 1024, 512)


def _sc_gate_kernel(gb_ref, gc_ref, h_ref, w_ref, o_ref):
    u = gc_ref[...].astype(F32) * h_ref[...].astype(F32)
    w = w_ref[...]
    kw = w.shape[0]
    acc = u * w[kw - 1:kw, :]
    for k in range(kw - 1):
        acc = acc + _shift_rows(u, kw - 1 - k) * w[k:k + 1, :]
    o_ref[...] = (gb_ref[...].astype(F32) * acc).astype(o_ref.dtype)


def _sc_gate(bch, conv_w, batch, seq, d, tc=128):
    t = bch.shape[0]
    nj = d // tc
    return pl.pallas_call(
        _sc_gate_kernel,
        grid=(batch, nj),
        in_specs=[pl.BlockSpec((seq, tc), lambda b, j: (b, j)),
                  pl.BlockSpec((seq, tc), lambda b, j: (b, nj + j)),
                  pl.BlockSpec((seq, tc), lambda b, j: (b, 2 * nj + j)),
                  pl.BlockSpec((conv_w.shape[0], tc), lambda b, j: (0, j))],
        out_specs=pl.BlockSpec((seq, tc), lambda b, j: (b, j)),
        out_shape=jax.ShapeDtypeStruct((t, d), BF16),
        compiler_params=_cparams("parallel", "parallel"),
        name="shortconv_gate",
    )(bch, bch, bch, conv_w.astype(F32))


def _shortconv_mixer(x16, in_proj, conv_w, out_proj, batch, seq):
    d = x16.shape[1]
    bch = _matmul_w32(x16, in_proj, BF16, 1024, 512)
    y = _sc_gate(bch, conv_w, batch, seq, d)
    return _matmul_w32(y, out_proj, BF16, 1024, 512)


def _top_values(s, k, rows):
    vals = []
    cur = s
    for _ in range(k):
        m = jnp.max(cur, axis=0, keepdims=True)
        vals.append(m)
        cur = jnp.where(cur == m, -jnp.inf, cur)
    if rows > k:
        vals.append(jnp.full((rows - k, s.shape[1]), -jnp.inf, F32))
    return jnp.concatenate(vals, axis=0)


def _peer_topk_kernel(qt_ref, keys_ref, s1c_ref, s2_ref, tauc_ref):
    dh = PEER_HALF_DIM
    k = PEER_TOPK

    def head(h, carry):
        r0 = pl.multiple_of(h * 2 * dh, 2 * dh)
        s1 = _dot(keys_ref[h, 0], qt_ref[pl.ds(r0, dh), :]) * LOG2E
        s2 = _dot(keys_ref[h, 1], qt_ref[pl.ds(r0 + dh, dh), :]) * LOG2E
        sa = _top_values(s1, k + 1, 24)
        sb = _top_values(s2, k + 1, 24)
        pieces = [sa[0:1, :] + sb]
        for r in range(1, 8):
            pieces.append(sa[r:r + 1, :] + sb[0:8, :])
        pieces.append(sa[8:24, :] + sb[0:1, :])
        top = _top_values(jnp.concatenate(pieces, axis=0), k + 1, k + 1)
        mx = top[0:1, :]
        z = jnp.sum(jnp.exp2(top[0:k, :] - mx), axis=0, keepdims=True)
        cst = mx + jnp.log(z) * LOG2E
        tau = 0.5 * (top[k - 1:k, :] + top[k:k + 1, :])
        s1c_ref[h] = s1 - cst
        s2_ref[h] = s2
        tauc_ref[h] = tau - cst
        return carry

    lax.fori_loop(0, PEER_HEADS, head, 0)


def _peer_topk(qt, keys16, tm=256):
    t = qt.shape[1]
    tm = _tile(t, tm)
    sc_shape = jax.ShapeDtypeStruct((PEER_HEADS, PEER_NKEYS, t), F32)
    row_shape = jax.ShapeDtypeStruct((PEER_HEADS, 1, t), F32)
    sc_spec = pl.BlockSpec((PEER_HEADS, PEER_NKEYS, tm), lambda i: (0, 0, i))
    row_spec = pl.BlockSpec((PEER_HEADS, 1, tm), lambda i: (0, 0, i))
    return pl.pallas_call(
        _peer_topk_kernel,
        grid=(t // tm,),
        in_specs=[pl.BlockSpec((qt.shape[0], tm), lambda i: (0, i)),
                  pl.BlockSpec(keys16.shape, lambda i: (0, 0, 0, 0))],
        out_specs=[sc_spec, sc_spec, row_spec],
        out_shape=[sc_shape, sc_shape, row_shape],
        compiler_params=_cparams("parallel"),
        name="peer_topk",
    )(qt, keys16)


def _peer_dense_kernel(xt_ref, u_ref, vt_ref, s1c_ref, s2_ref, tauc_ref, o_ref,
                       h_ref, act_ref, s1b_ref, taub_ref, *, te, ne):
    g = pl.program_id(0)
    e1 = jnp.maximum(g - 1, 0) % ne
    e2 = jnp.maximum(g - 2, 0) % ne
    new = g % 2
    old = 1 - new
    nk = PEER_NKEYS
    ni = te // nk
    d, tm = o_ref.shape
    gp = s1b_ref.shape[0]
    db = d // (ni // gp)
    rb = 2 * SUBLANES

    @pl.when(g == 0)
    def _():
        h_ref[...] = jnp.zeros_like(h_ref)
        act_ref[...] = jnp.zeros_like(act_ref)

    @pl.when(e2 == 0)
    def _():
        o_ref[...] = jnp.zeros_like(o_ref)

    @pl.when(e1 == 0)
    def _():
        for h in range(PEER_HEADS):
            taub_ref[h] = jnp.broadcast_to(tauc_ref[h], (SUBLANES, tm))

    h_ref[new] = _dot(u_ref[...], xt_ref[...])

    def piece(k, carry):
        rows = pl.ds(pl.multiple_of(k * db, db), db)
        o_ref[rows, :] += _dot(vt_ref[rows, :], act_ref[old])
        for q in range(gp):
            kq = k * gp + q
            for h in range(PEER_HEADS):
                s1b_ref[q, h] = jnp.broadcast_to(s1c_ref[h, pl.ds(e1 * ni + kq, 1), :], (SUBLANES, tm))
            for jb in range(nk // rb):
                halves = []
                for j0 in (jb * rb, jb * rb + SUBLANES):
                    gate = None
                    for h in range(PEER_HEADS):
                        a = s1b_ref[q, h] + s2_ref[h, j0:j0 + SUBLANES, :]
                        gh = jnp.where(a >= taub_ref[h], jnp.exp2(a), 0.0)
                        gate = gh if gate is None else gate + gh
                    halves.append(gate)
                r0 = pl.multiple_of(kq * nk, nk) + jb * rb
                pre = h_ref[old, pl.ds(r0, rb), :]
                act = 0.5 * pre * (1.0 + lax.erf(pre * (1.0 / math.sqrt(2.0))))
                act_ref[new, pl.ds(r0, rb), :] = (act * jnp.concatenate(halves, axis=0)).astype(BF16)
        return carry

    lax.fori_loop(0, ni // gp, piece, 0)


def _peer_dense(xt, u16, vt16, s1c, s2, tauc, tm=512, te=512, groups_per_piece=1):
    d, t = xt.shape
    nexp = u16.shape[0]
    tm, te = _tile(t, tm), _tile(nexp, te)
    assert te % (PEER_NKEYS * groups_per_piece) == 0 and d % (te // PEER_NKEYS) == 0
    ne = nexp // te
    npairs = (t // tm) * ne
    pair = lambda g, lag: jnp.clip(g - lag, 0, npairs - 1)
    sc_spec = pl.BlockSpec((PEER_HEADS, PEER_NKEYS, tm), lambda g: (0, 0, pair(g, 1) // ne))
    row_spec = pl.BlockSpec((PEER_HEADS, 1, tm), lambda g: (0, 0, pair(g, 1) // ne))
    return pl.pallas_call(
        functools.partial(_peer_dense_kernel, te=te, ne=ne),
        grid=(npairs + 2,),
        in_specs=[pl.BlockSpec((d, tm), lambda g: (0, pair(g, 0) // ne)),
                  pl.BlockSpec((te, d), lambda g: (pair(g, 0) % ne, 0)),
                  pl.BlockSpec((d, te), lambda g: (0, pair(g, 2) % ne)),
                  sc_spec, sc_spec, row_spec],
        out_specs=pl.BlockSpec((d, tm), lambda g: (0, pair(g, 2) // ne), pipeline_mode=pl.Buffered(1)),
        out_shape=jax.ShapeDtypeStruct((d, t), F32),
        scratch_shapes=[pltpu.VMEM((2, te, tm), F32),
                        pltpu.VMEM((2, te, tm), BF16),
                        pltpu.VMEM((groups_per_piece, PEER_HEADS, SUBLANES, tm), F32),
                        pltpu.VMEM((PEER_HEADS, SUBLANES, tm), F32)],
        compiler_params=_cparams("arbitrary"),
        name="peer_dense",
    )(xt, u16, vt16, s1c, s2, tauc)


def _peer_ffn(xt, wq, subkeys, u, v):
    qt = _matmul(wq.T.astype(BF16), xt, BF16, 1024, 1024)
    s1c, s2, tauc = _peer_topk(qt, subkeys.astype(BF16))
    return _peer_dense(xt, u.astype(BF16), v.T.astype(BF16), s1c, s2, tauc)


def kernel(x, ssd_in_proj, ssd_conv_w, ssd_conv_b, ssd_dt_bias, ssd_A_log, ssd_D, ssd_norm_w,
           ssd_out_proj, sc_in_proj, sc_conv_w, sc_out_proj, peer_wq, peer_subkeys, peer_u,
           peer_v, ln1_g, ln1_b, ln2_g, ln2_b):
    batch, seq, d = x.shape
    depth = peer_wq.shape[0]
    alpha = (2 * depth) ** 0.25
    x32 = x.reshape(batch * seq, d).astype(F32)
    x16 = x32.astype(BF16)
    for i in range(depth):
        j = i // 2
        if i % 2 == 0:
            mix = _ssd_mixer(x16, ssd_in_proj[j], ssd_conv_w[j], ssd_conv_b[j], ssd_dt_bias[j],
                             ssd_A_log[j], ssd_D[j], ssd_norm_w[j], ssd_out_proj[j], batch, seq)
        else:
            mix = _shortconv_mixer(x16, sc_in_proj[j], sc_conv_w[j], sc_out_proj[j], batch, seq)
        x32, xt16 = _residual_ln(x32, mix, ln1_g[i], ln1_b[i], alpha, transposed=False)
        ffn_t = _peer_ffn(xt16, peer_wq[i], peer_subkeys[i], peer_u[i], peer_v[i])
        x32, x16 = _residual_ln(x32, ffn_t, ln2_g[i], ln2_b[i], alpha, transposed=True)
    return x32.reshape(batch, seq, d).astype(x.dtype)
```

```python
import functools
import math

import jax
import jax.numpy as jnp
from jax import lax
from jax.experimental import pallas as pl
from jax.experimental.pallas import tpu as pltpu

F32 = jnp.float32
BF16 = jnp.bfloat16

CHUNK = 64
SSD_HEADDIM = 64
SSD_GROUPS = 8
SSD_STATE = 128
PEER_HEADS = 8
PEER_NKEYS = 128
PEER_TOPK = 16
PEER_HALF_DIM = 128
LN_EPS = 1e-5
RMS_EPS = 1e-5
LOG2E = 1.4426950408889634

SUBLANES = 8
VMEM_LIMIT_BYTES = 56 * 1024 * 1024


def _cparams(*sem):
    return pltpu.CompilerParams(dimension_semantics=sem, vmem_limit_bytes=VMEM_LIMIT_BYTES)


def _tile(n, pref):
    t = min(pref, n)
    while n % t:
        t //= 2
    return t


def _split2(x):
    hi = x.astype(BF16)
    lo = (x - hi.astype(F32)).astype(BF16)
    return hi, lo


def _dot(a, b):
    return jnp.dot(a, b, preferred_element_type=F32)


def _dot2(x, sel):
    hi, lo = _split2(x)
    return _dot(hi, sel) + _dot(lo, sel)


def _mm_kernel(a_ref, w_ref, o_ref):
    o_ref[...] = _dot(a_ref[...], w_ref[...]).astype(o_ref.dtype)


def _matmul(a, w, out_dtype, tm, tn):
    m, k = a.shape
    n = w.shape[1]
    tm, tn = _tile(m, tm), _tile(n, tn)
    return pl.pallas_call(
        _mm_kernel,
        grid=(m // tm, n // tn),
        in_specs=[pl.BlockSpec((tm, k), lambda i, j: (i, 0)),
                  pl.BlockSpec((k, tn), lambda i, j: (0, j))],
        out_specs=pl.BlockSpec((tm, tn), lambda i, j: (i, j)),
        out_shape=jax.ShapeDtypeStruct((m, n), out_dtype),
        compiler_params=_cparams("parallel", "arbitrary"),
        name="matmul",
    )(a, w)


def _mm_wcast_kernel(a_ref, w_ref, o_ref, w16_ref):
    @pl.when(pl.program_id(1) == 0)
    def _():
        w16_ref[...] = w_ref[...].astype(BF16)

    o_ref[...] = _dot(a_ref[...], w16_ref[...]).astype(o_ref.dtype)


def _matmul_w32(a, w32, layer, out_dtype, tm, tn, n=None):
    m, k = a.shape
    n = w32.shape[2] if n is None else n
    tm, tn = _tile(m, tm), _tile(n, tn)
    return pl.pallas_call(
        _mm_wcast_kernel,
        grid=(n // tn, m // tm),
        in_specs=[pl.BlockSpec((tm, k), lambda j, i: (i, 0)),
                  pl.BlockSpec((None, k, tn), lambda j, i: (layer, 0, j))],
        out_specs=pl.BlockSpec((tm, tn), lambda j, i: (i, j)),
        out_shape=jax.ShapeDtypeStruct((m, n), out_dtype),
        scratch_shapes=[pltpu.VMEM((k, tn), BF16)],
        compiler_params=_cparams("parallel", "arbitrary"),
        name="matmul_w32",
    )(a, w32)


def _ln_kernel(x_ref, y_ref, g_ref, b_ref, o32_ref, o16_ref, *, alpha, transposed):
    y = y_ref[...].astype(F32)
    if transposed:
        y = y.T
    h = alpha * x_ref[...] + y
    mu = jnp.mean(h, axis=-1, keepdims=True)
    d = h - mu
    var = jnp.mean(d * d, axis=-1, keepdims=True)
    out = d * lax.rsqrt(var + LN_EPS) * g_ref[...] + b_ref[...]
    o32_ref[...] = out
    if transposed:
        o16_ref[...] = out.astype(BF16)
    else:
        o16_ref[...] = out.T.astype(BF16)


def _residual_ln(x, y, g, b, alpha, transposed, tm=256):
    t, d = x.shape
    tm = _tile(t, tm)
    row = pl.BlockSpec((tm, d), lambda i: (i, 0))
    col = pl.BlockSpec((d, tm), lambda i: (0, i))
    vec = pl.BlockSpec((1, d), lambda i: (0, 0))
    return pl.pallas_call(
        functools.partial(_ln_kernel, alpha=alpha, transposed=transposed),
        grid=(t // tm,),
        in_specs=[row, col if transposed else row, vec, vec],
        out_specs=[row, row if transposed else col],
        out_shape=[jax.ShapeDtypeStruct((t, d), F32),
                   jax.ShapeDtypeStruct((t, d) if transposed else (d, t), BF16)],
        compiler_params=_cparams("parallel"),
        name="residual_ln",
    )(x, y, g.reshape(1, d).astype(F32), b.reshape(1, d).astype(F32))


def _shift_rows(h, k):
    rolled = pltpu.roll(h, k, 0)
    rows = lax.broadcasted_iota(jnp.int32, h.shape, 0)
    return jnp.where(rows >= k, rolled, 0.0)


def _ssd_conv_kernel(x_ref, w_ref, b_ref, o_ref):
    h = x_ref[...].astype(F32)
    w = w_ref[...]
    kw = w.shape[0]
    acc = h * w[kw - 1:kw, :] + b_ref[...]
    for k in range(kw - 1):
        acc = acc + _shift_rows(h, kw - 1 - k) * w[k:k + 1, :]
    o_ref[...] = (acc * jax.nn.sigmoid(acc)).astype(o_ref.dtype)


def _ssd_conv(zx, conv_w, conv_b, batch, seq, d_inner, tc=128):
    t = zx.shape[0]
    cdim = conv_w.shape[1]
    off = d_inner // tc
    return pl.pallas_call(
        _ssd_conv_kernel,
        grid=(batch, cdim // tc),
        in_specs=[pl.BlockSpec((seq, tc), lambda b, j: (b, off + j)),
                  pl.BlockSpec((conv_w.shape[0], tc), lambda b, j: (0, j)),
                  pl.BlockSpec((1, tc), lambda b, j: (0, j))],
        out_specs=pl.BlockSpec((seq, tc), lambda b, j: (b, j)),
        out_shape=jax.ShapeDtypeStruct((t, cdim), BF16),
        compiler_params=_cparams("parallel", "parallel"),
        name="ssd_conv_silu",
    )(zx, conv_w.astype(F32), conv_b.reshape(1, cdim).astype(F32))


def _ssd_scan_kernel(xs_ref, b_ref, c_ref, z_ref, dtraw_ref, dtb_ref, a_ref, sel_ref,
                     dskip_ref, nw_ref, o_ref, state_ref, dte_ref, acse_ref, *, ts, gw):
    nchunk = ts // CHUNK

    @pl.when(pl.program_id(2) == 0)
    def _():
        state_ref[...] = jnp.zeros_like(state_ref)

    xr = dtraw_ref[...] + dtb_ref[...]
    dt = jnp.maximum(xr, 0.0) + jnp.log1p(jnp.exp(-jnp.abs(xr)))
    da = dt * a_ref[...]
    ri = lax.broadcasted_iota(jnp.int32, (ts, ts), 0)
    ci = lax.broadcasted_iota(jnp.int32, (ts, ts), 1)
    tril = ((ri >= ci) & (ri // CHUNK == ci // CHUNK)).astype(BF16)
    h1, h2 = _split2(da)
    acs = _dot(tril, h1) + _dot(tril, h2)
    sel = sel_ref[...]
    dte_ref[...] = _dot2(dt, sel)
    acse_ref[...] = _dot2(acs, sel)

    li = lax.broadcasted_iota(jnp.int32, (CHUNK, gw), 0)
    si = lax.broadcasted_iota(jnp.int32, (CHUNK, gw), 1) % CHUNK
    diag = li == si
    causal = li >= si
    r2 = lax.broadcasted_iota(jnp.int32, (2 * CHUNK, 2 * CHUNK), 0) // CHUNK
    c2 = lax.broadcasted_iota(jnp.int32, (2 * CHUNK, 2 * CHUNK), 1) // CHUNK
    blockdiag = r2 == c2
    dskip = dskip_ref[...]
    nw = nw_ref[...]

    def chunk(c, carry):
        rows = pl.ds(pl.multiple_of(c * CHUNK, CHUNK), CHUNK)
        cc = c_ref[rows, :]
        bc = b_ref[rows, :]
        xs = xs_ref[rows, :].astype(F32)
        dte = dte_ref[rows, :]
        acse = acse_ref[rows, :]
        xdt = xs * dte
        acs_last = acse[CHUNK - 1:CHUNK, :]
        acs_t = jnp.sum(jnp.where(diag, acse, 0.0), axis=0, keepdims=True)
        decay = jnp.exp2(jnp.where(causal, acse - acs_t, -jnp.inf))
        bdup = jnp.concatenate([bc, bc], axis=0)
        cb2 = lax.dot_general(cc, bdup, (((1,), (1,)), ((), ())), preferred_element_type=F32)
        xdt16 = xdt.astype(BF16)
        ys = []
        for q in range(gw // (2 * CHUNK)):
            lanes = slice(q * 2 * CHUNK, (q + 1) * 2 * CHUNK)
            wq = (cb2 * decay[:, lanes]).astype(BF16)
            xq = xdt16[:, lanes]
            bd = jnp.where(blockdiag, jnp.concatenate([xq, xq], axis=0), jnp.zeros((), BF16))
            ys.append(_dot(wq, bd))
        y = jnp.concatenate(ys, axis=1) if len(ys) > 1 else ys[0]
        state = state_ref[...]
        y = y + _dot(cc, state.astype(BF16)) * jnp.exp2(acse)
        xdtw = (xdt * jnp.exp2(acs_last - acse)).astype(BF16)
        state_ref[...] = state * jnp.exp2(acs_last) + lax.dot_general(
            bc, xdtw, (((0,), (0,)), ((), ())), preferred_element_type=F32)
        y = y + dskip * xs
        zc = z_ref[rows, :].astype(F32)
        hh = y * (zc * jax.nn.sigmoid(zc))
        ms = jnp.mean(hh * hh, axis=-1, keepdims=True)
        o_ref[rows, :] = (hh * lax.rsqrt(ms + RMS_EPS) * nw).astype(o_ref.dtype)
        return carry

    lax.fori_loop(0, nchunk, chunk, 0, unroll=2)


def _ssd_scan(zx, xbc, dt_raw, dt_bias, a_log, d_skip, norm_w, batch, seq, d_inner, ts=512):
    t = zx.shape[0]
    nheads = dt_raw.shape[1]
    gw = d_inner // SSD_GROUPS
    hpg = nheads // SSD_GROUPS
    assert gw % (2 * CHUNK) == 0 and gw == hpg * SSD_HEADDIM
    ts = min(ts, seq)
    spb = seq // ts
    nb = SSD_STATE // 128
    head_of_lane = jnp.arange(d_inner, dtype=jnp.int32) // SSD_HEADDIM
    sel = (jnp.arange(nheads, dtype=jnp.int32)[:, None] == head_of_lane[None, :]).astype(BF16)
    dskip = jnp.repeat(d_skip.astype(F32), SSD_HEADDIM).reshape(1, d_inner)
    a = -jnp.exp(a_log.astype(F32)).reshape(1, nheads) * LOG2E
    kern = functools.partial(_ssd_scan_kernel, ts=ts, gw=gw)
    boff = d_inner // SSD_STATE
    return pl.pallas_call(
        kern,
        grid=(batch, SSD_GROUPS, spb),
        in_specs=[
            pl.BlockSpec((ts, gw), lambda b, g, s: (b * spb + s, g)),
            pl.BlockSpec((ts, SSD_STATE), lambda b, g, s: (b * spb + s, boff + g * nb)),
            pl.BlockSpec((ts, SSD_STATE), lambda b, g, s: (b * spb + s, boff + (SSD_GROUPS + g) * nb)),
            pl.BlockSpec((ts, gw), lambda b, g, s: (b * spb + s, g)),
            pl.BlockSpec((ts, nheads), lambda b, g, s: (b * spb + s, 0)),
            pl.BlockSpec((1, nheads), lambda b, g, s: (0, 0)),
            pl.BlockSpec((1, nheads), lambda b, g, s: (0, 0)),
            pl.BlockSpec((nheads, gw), lambda b, g, s: (0, g)),
            pl.BlockSpec((1, gw), lambda b, g, s: (0, g)),
            pl.BlockSpec((1, gw), lambda b, g, s: (0, g)),
        ],
        out_specs=pl.BlockSpec((ts, gw), lambda b, g, s: (b * spb + s, g)),
        out_shape=jax.ShapeDtypeStruct((t, d_inner), BF16),
        scratch_shapes=[pltpu.VMEM((SSD_STATE, gw), F32),
                        pltpu.VMEM((ts, gw), F32),
                        pltpu.VMEM((ts, gw), F32)],
        compiler_params=_cparams("parallel", "parallel", "arbitrary"),
        name="ssd_scan",
    )(xbc, xbc, xbc, zx, dt_raw, dt_bias.reshape(1, nheads).astype(F32), a, sel, dskip,
      norm_w.reshape(1, d_inner).astype(F32))


def _ssd_mixer(x16, in_proj, layer, conv_w, conv_b, dt_bias, a_log, d_skip, norm_w, out_proj, batch, seq):
    d_inner = out_proj.shape[0]
    cdim = conv_w.shape[1]
    zx = _matmul_w32(x16, in_proj, layer, BF16, 1024, 512, n=d_inner + cdim)
    dt_raw = _matmul(x16, in_proj[layer, :, d_inner + cdim:].astype(BF16), F32, 1024, 128)
    xbc = _ssd_conv(zx, conv_w, conv_b, batch, seq, d_inner)
    h = _ssd_scan(zx, xbc, dt_raw, dt_bias, a_log, d_skip, norm_w, batch, seq, d_inner)
    return _matmul(h, out_proj.astype(BF16), BF16, 1024, 512)


def _sc_gate_kernel(gb_ref, gc_ref, h_ref, w_ref, o_ref):
    u = gc_ref[...].astype(F32) * h_ref[...].astype(F32)
    w = w_ref[...]
    kw = w.shape[0]
    acc = u * w[kw - 1:kw, :]
    for k in range(kw - 1):
        acc = acc + _shift_rows(u, kw - 1 - k) * w[k:k + 1, :]
    o_ref[...] = (gb_ref[...].astype(F32) * acc).astype(o_ref.dtype)


def _sc_gate(bch, conv_w, batch, seq, d, tc=128):
    t = bch.shape[0]
    nj = d // tc
    return pl.pallas_call(
        _sc_gate_kernel,
        grid=(batch, nj),
        in_specs=[pl.BlockSpec((seq, tc), lambda b, j: (b, j)),
                  pl.BlockSpec((seq, tc), lambda b, j: (b, nj + j)),
                  pl.BlockSpec((seq, tc), lambda b, j: (b, 2 * nj + j)),
                  pl.BlockSpec((conv_w.shape[0], tc), lambda b, j: (0, j))],
        out_specs=pl.BlockSpec((seq, tc), lambda b, j: (b, j)),
        out_shape=jax.ShapeDtypeStruct((t, d), BF16),
        compiler_params=_cparams("parallel", "parallel"),
        name="shortconv_gate",
    )(bch, bch, bch, conv_w.astype(F32))


def _shortconv_mixer(x16, in_proj, out_proj, layer, conv_w, batch, seq):
    d = x16.shape[1]
    bch = _matmul_w32(x16, in_proj, layer, BF16, 1024, 512)
    y = _sc_gate(bch, conv_w, batch, seq, d)
    return _matmul_w32(y, out_proj, layer, BF16, 1024, 512)


def _top_values(s, k, rows):
    vals = []
    cur = s
    for _ in range(k):
        m = jnp.max(cur, axis=0, keepdims=True)
        vals.append(m)
        cur = jnp.where(cur == m, -jnp.inf, cur)
    if rows > k:
        vals.append(jnp.full((rows - k, s.shape[1]), -jnp.inf, F32))
    return jnp.concatenate(vals, axis=0)


def _peer_topk_kernel(qt_ref, keys_ref, s1c_ref, s2_ref, tauc_ref):
    dh = PEER_HALF_DIM
    k = PEER_TOPK

    def head(h, carry):
        r0 = pl.multiple_of(h * 2 * dh, 2 * dh)
        s1 = _dot(keys_ref[h, 0], qt_ref[pl.ds(r0, dh), :]) * LOG2E
        s2 = _dot(keys_ref[h, 1], qt_ref[pl.ds(r0 + dh, dh), :]) * LOG2E
        sa = _top_values(s1, k + 1, 24)
        sb = _top_values(s2, k + 1, 24)
        pieces = [sa[0:1, :] + sb]
        for r in range(1, 8):
            pieces.append(sa[r:r + 1, :] + sb[0:8, :])
        pieces.append(sa[8:24, :] + sb[0:1, :])
        top = _top_values(jnp.concatenate(pieces, axis=0), k + 1, k + 1)
        mx = top[0:1, :]
        z = jnp.sum(jnp.exp2(top[0:k, :] - mx), axis=0, keepdims=True)
        cst = mx + jnp.log(z) * LOG2E
        tau = 0.5 * (top[k - 1:k, :] + top[k:k + 1, :])
        s1c_ref[h] = s1 - cst
        s2_ref[h] = s2
        tauc_ref[h] = tau - cst
        return carry

    lax.fori_loop(0, PEER_HEADS, head, 0)


def _peer_topk(qt, keys16, tm=256):
    t = qt.shape[1]
    tm = _tile(t, tm)
    sc_shape = jax.ShapeDtypeStruct((PEER_HEADS, PEER_NKEYS, t), F32)
    row_shape = jax.ShapeDtypeStruct((PEER_HEADS, 1, t), F32)
    sc_spec = pl.BlockSpec((PEER_HEADS, PEER_NKEYS, tm), lambda i: (0, 0, i))
    row_spec = pl.BlockSpec((PEER_HEADS, 1, tm), lambda i: (0, 0, i))
    return pl.pallas_call(
        _peer_topk_kernel,
        grid=(t // tm,),
        in_specs=[pl.BlockSpec((qt.shape[0], tm), lambda i: (0, i)),
                  pl.BlockSpec(keys16.shape, lambda i: (0, 0, 0, 0))],
        out_specs=[sc_spec, sc_spec, row_spec],
        out_shape=[sc_shape, sc_shape, row_shape],
        compiler_params=_cparams("parallel"),
        name="peer_topk",
    )(qt, keys16)


def _peer_dense_kernel(xt_ref, u_ref, vt_ref, s1c_ref, s2_ref, tauc_ref, o_ref,
                       h_ref, act_ref, s1b_ref, taub_ref, *, te, ne):
    g = pl.program_id(0)
    e1 = jnp.maximum(g - 1, 0) % ne
    e2 = jnp.maximum(g - 2, 0) % ne
    new = g % 2
    old = 1 - new
    nk = PEER_NKEYS
    ni = te // nk
    d, tm = o_ref.shape
    gp = s1b_ref.shape[0]
    db = d // (ni // gp)
    rb = 2 * SUBLANES

    @pl.when(g == 0)
    def _():
        h_ref[...] = jnp.zeros_like(h_ref)
        act_ref[...] = jnp.zeros_like(act_ref)

    @pl.when(e2 == 0)
    def _():
        o_ref[...] = jnp.zeros_like(o_ref)

    @pl.when(e1 == 0)
    def _():
        for h in range(PEER_HEADS):
            taub_ref[h] = jnp.broadcast_to(tauc_ref[h], (SUBLANES, tm))

    h_ref[new] = _dot(u_ref[...], xt_ref[...])

    def piece(k, carry):
        rows = pl.ds(pl.multiple_of(k * db, db), db)
        o_ref[rows, :] += _dot(vt_ref[rows, :], act_ref[old])
        for q in range(gp):
            kq = k * gp + q
            for h in range(PEER_HEADS):
                s1b_ref[q, h] = jnp.broadcast_to(s1c_ref[h, pl.ds(e1 * ni + kq, 1), :], (SUBLANES, tm))
            for jb in range(nk // rb):
                halves = []
                for j0 in (jb * rb, jb * rb + SUBLANES):
                    gate = None
                    for h in range(PEER_HEADS):
                        a = s1b_ref[q, h] + s2_ref[h, j0:j0 + SUBLANES, :]
                        gh = jnp.where(a >= taub_ref[h], jnp.exp2(a), 0.0)
                        gate = gh if gate is None else gate + gh
                    halves.append(gate)
                r0 = pl.multiple_of(kq * nk, nk) + jb * rb
                pre = h_ref[old, pl.ds(r0, rb), :]
                act = 0.5 * pre * (1.0 + lax.erf(pre * (1.0 / math.sqrt(2.0))))
                act_ref[new, pl.ds(r0, rb), :] = (act * jnp.concatenate(halves, axis=0)).astype(BF16)
        return carry

    lax.fori_loop(0, ni // gp, piece, 0)


def _peer_dense(xt, u16, vt16, s1c, s2, tauc, tm=512, te=512, groups_per_piece=1):
    d, t = xt.shape
    nexp = u16.shape[0]
    tm, te = _tile(t, tm), _tile(nexp, te)
    assert te % (PEER_NKEYS * groups_per_piece) == 0 and d % (te // PEER_NKEYS) == 0
    ne = nexp // te
    npairs = (t // tm) * ne
    pair = lambda g, lag: jnp.clip(g - lag, 0, npairs - 1)
    sc_spec = pl.BlockSpec((PEER_HEADS, PEER_NKEYS, tm), lambda g: (0, 0, pair(g, 1) // ne))
    row_spec = pl.BlockSpec((PEER_HEADS, 1, tm), lambda g: (0, 0, pair(g, 1) // ne))
    return pl.pallas_call(
        functools.partial(_peer_dense_kernel, te=te, ne=ne),
        grid=(npairs + 2,),
        in_specs=[pl.BlockSpec((d, tm), lambda g: (0, pair(g, 0) // ne)),
                  pl.BlockSpec((te, d), lambda g: (pair(g, 0) % ne, 0)),
                  pl.BlockSpec((d, te), lambda g: (0, pair(g, 2) % ne)),
                  sc_spec, sc_spec, row_spec],
        out_specs=pl.BlockSpec((d, tm), lambda g: (0, pair(g, 2) // ne), pipeline_mode=pl.Buffered(1)),
        out_shape=jax.ShapeDtypeStruct((d, t), F32),
        scratch_shapes=[pltpu.VMEM((2, te, tm), F32),
                        pltpu.VMEM((2, te, tm), BF16),
                        pltpu.VMEM((groups_per_piece, PEER_HEADS, SUBLANES, tm), F32),
                        pltpu.VMEM((PEER_HEADS, SUBLANES, tm), F32)],
        compiler_params=_cparams("arbitrary"),
        name="peer_dense",
    )(xt, u16, vt16, s1c, s2, tauc)


def _peer_ffn(xt, wq, subkeys, u, v):
    qt = _matmul(wq.T.astype(BF16), xt, BF16, 1024, 1024)
    s1c, s2, tauc = _peer_topk(qt, subkeys.astype(BF16))
    return _peer_dense(xt, u.astype(BF16), v.T.astype(BF16), s1c, s2, tauc)


def kernel(x, ssd_in_proj, ssd_conv_w, ssd_conv_b, ssd_dt_bias, ssd_A_log, ssd_D, ssd_norm_w,
           ssd_out_proj, sc_in_proj, sc_conv_w, sc_out_proj, peer_wq, peer_subkeys, peer_u,
           peer_v, ln1_g, ln1_b, ln2_g, ln2_b):
    batch, seq, d = x.shape
    depth = peer_wq.shape[0]
    alpha = (2 * depth) ** 0.25
    x32 = x.reshape(batch * seq, d).astype(F32)
    x16 = x32.astype(BF16)
    for i in range(depth):
        j = i // 2
        if i % 2 == 0:
            mix = _ssd_mixer(x16, ssd_in_proj, j, ssd_conv_w[j], ssd_conv_b[j], ssd_dt_bias[j],
                             ssd_A_log[j], ssd_D[j], ssd_norm_w[j], ssd_out_proj[j], batch, seq)
        else:
            mix = _shortconv_mixer(x16, sc_in_proj, sc_out_proj, j, sc_conv_w[j], batch, seq)
        x32, xt16 = _residual_ln(x32, mix, ln1_g[i], ln1_b[i], alpha, transposed=False)
        ffn_t = _peer_ffn(xt16, peer_wq[i], peer_subkeys[i], peer_u[i], peer_v[i])
        x32, x16 = _residual_ln(x32, ffn_t, ln2_g[i], ln2_b[i], alpha, transposed=True)
    return x32.reshape(batch, seq, d).astype(x.dtype)
```

```python
import functools
import math

import jax
import jax.numpy as jnp
from jax import lax
from jax.experimental import pallas as pl
from jax.experimental.pallas import tpu as pltpu

F32 = jnp.float32
BF16 = jnp.bfloat16

CHUNK = 64
SSD_HEADDIM = 64
SSD_GROUPS = 8
SSD_STATE = 128
PEER_HEADS = 8
PEER_NKEYS = 128
PEER_TOPK = 16
PEER_HALF_DIM = 128
LN_EPS = 1e-5
RMS_EPS = 1e-5
LOG2E = 1.4426950408889634

SUBLANES = 8
VMEM_LIMIT_BYTES = 56 * 1024 * 1024


def _cparams(*sem):
    return pltpu.CompilerParams(dimension_semantics=sem, vmem_limit_bytes=VMEM_LIMIT_BYTES)


def _tile(n, pref):
    t = min(pref, n)
    while n % t:
        t //= 2
    return t


def _split2(x):
    hi = x.astype(BF16)
    lo = (x - hi.astype(F32)).astype(BF16)
    return hi, lo


def _dot(a, b):
    return jnp.dot(a, b, preferred_element_type=F32)


def _dot2(x, sel):
    hi, lo = _split2(x)
    return _dot(hi, sel) + _dot(lo, sel)


def _mm_kernel(a_ref, w_ref, o_ref):
    o_ref[...] = _dot(a_ref[...], w_ref[...]).astype(o_ref.dtype)


def _matmul(a, w, out_dtype, tm, tn, a_layer=None, w_layer=None):
    m, k = a.shape[-2:]
    n = w.shape[-1]
    tm, tn = _tile(m, tm), _tile(n, tn)
    a_spec = (pl.BlockSpec((tm, k), lambda i, j: (i, 0)) if a_layer is None else
              pl.BlockSpec((None, tm, k), lambda i, j: (a_layer, i, 0)))
    w_spec = (pl.BlockSpec((k, tn), lambda i, j: (0, j)) if w_layer is None else
              pl.BlockSpec((None, k, tn), lambda i, j: (w_layer, 0, j)))
    return pl.pallas_call(
        _mm_kernel,
        grid=(m // tm, n // tn),
        in_specs=[a_spec, w_spec],
        out_specs=pl.BlockSpec((tm, tn), lambda i, j: (i, j)),
        out_shape=jax.ShapeDtypeStruct((m, n), out_dtype),
        compiler_params=_cparams("parallel", "arbitrary"),
        name="matmul",
    )(a, w)


def _mm_wcast_kernel(a_ref, w_ref, o_ref, w16_ref):
    @pl.when(pl.program_id(1) == 0)
    def _():
        w16_ref[...] = w_ref[...].astype(BF16)

    o_ref[...] = _dot(a_ref[...], w16_ref[...]).astype(o_ref.dtype)


def _matmul_w32(a, w32, layer, out_dtype, tm, tn, col0=0, n=None):
    m, k = a.shape
    n = w32.shape[2] - col0 if n is None else n
    tm, tn = _tile(m, tm), _tile(n, tn)
    assert col0 % tn == 0
    j0 = col0 // tn
    return pl.pallas_call(
        _mm_wcast_kernel,
        grid=(n // tn, m // tm),
        in_specs=[pl.BlockSpec((tm, k), lambda j, i: (i, 0)),
                  pl.BlockSpec((None, k, tn), lambda j, i: (layer, 0, j0 + j))],
        out_specs=pl.BlockSpec((tm, tn), lambda j, i: (i, j)),
        out_shape=jax.ShapeDtypeStruct((m, n), out_dtype),
        scratch_shapes=[pltpu.VMEM((k, tn), BF16)],
        compiler_params=_cparams("parallel", "arbitrary"),
        name="matmul_w32",
    )(a, w32)


def _ln_kernel(x_ref, y_ref, g_ref, b_ref, o32_ref, o16_ref, *, alpha, transposed):
    y = y_ref[...].astype(F32)
    if transposed:
        y = y.T
    h = alpha * x_ref[...] + y
    mu = jnp.mean(h, axis=-1, keepdims=True)
    d = h - mu
    var = jnp.mean(d * d, axis=-1, keepdims=True)
    out = d * lax.rsqrt(var + LN_EPS) * g_ref[...] + b_ref[...]
    o32_ref[...] = out
    if transposed:
        o16_ref[...] = out.astype(BF16)
    else:
        o16_ref[...] = out.T.astype(BF16)


def _residual_ln(x, y, g, b, alpha, transposed, tm=256):
    t, d = x.shape
    tm = _tile(t, tm)
    row = pl.BlockSpec((tm, d), lambda i: (i, 0))
    col = pl.BlockSpec((d, tm), lambda i: (0, i))
    vec = pl.BlockSpec((1, d), lambda i: (0, 0))
    return pl.pallas_call(
        functools.partial(_ln_kernel, alpha=alpha, transposed=transposed),
        grid=(t // tm,),
        in_specs=[row, col if transposed else row, vec, vec],
        out_specs=[row, row if transposed else col],
        out_shape=[jax.ShapeDtypeStruct((t, d), F32),
                   jax.ShapeDtypeStruct((t, d) if transposed else (d, t), BF16)],
        compiler_params=_cparams("parallel"),
        name="residual_ln",
    )(x, y, g.reshape(1, d).astype(F32), b.reshape(1, d).astype(F32))


def _shift_rows(h, k):
    rolled = pltpu.roll(h, k, 0)
    rows = lax.broadcasted_iota(jnp.int32, h.shape, 0)
    return jnp.where(rows >= k, rolled, 0.0)


def _ssd_conv_kernel(x_ref, w_ref, b_ref, o_ref):
    h = x_ref[...].astype(F32)
    w = w_ref[...]
    kw = w.shape[0]
    acc = h * w[kw - 1:kw, :] + b_ref[...]
    for k in range(kw - 1):
        acc = acc + _shift_rows(h, kw - 1 - k) * w[k:k + 1, :]
    o_ref[...] = (acc * jax.nn.sigmoid(acc)).astype(o_ref.dtype)


def _ssd_conv(zx, conv_w, conv_b, batch, seq, d_inner, tc=128):
    t = zx.shape[0]
    cdim = conv_w.shape[1]
    off = d_inner // tc
    return pl.pallas_call(
        _ssd_conv_kernel,
        grid=(batch, cdim // tc),
        in_specs=[pl.BlockSpec((seq, tc), lambda b, j: (b, off + j)),
                  pl.BlockSpec((conv_w.shape[0], tc), lambda b, j: (0, j)),
                  pl.BlockSpec((1, tc), lambda b, j: (0, j))],
        out_specs=pl.BlockSpec((seq, tc), lambda b, j: (b, j)),
        out_shape=jax.ShapeDtypeStruct((t, cdim), BF16),
        compiler_params=_cparams("parallel", "parallel"),
        name="ssd_conv_silu",
    )(zx, conv_w.astype(F32), conv_b.reshape(1, cdim).astype(F32))


def _ssd_scan_kernel(xs_ref, b_ref, c_ref, z_ref, dtraw_ref, dtb_ref, a_ref, sel_ref,
                     dskip_ref, nw_ref, o_ref, state_ref, dte_ref, acse_ref, *, ts, gw):
    nchunk = ts // CHUNK

    @pl.when(pl.program_id(2) == 0)
    def _():
        state_ref[...] = jnp.zeros_like(state_ref)

    xr = dtraw_ref[...] + dtb_ref[...]
    dt = jnp.maximum(xr, 0.0) + jnp.log1p(jnp.exp(-jnp.abs(xr)))
    da = dt * a_ref[...]
    ri = lax.broadcasted_iota(jnp.int32, (ts, ts), 0)
    ci = lax.broadcasted_iota(jnp.int32, (ts, ts), 1)
    tril = ((ri >= ci) & (ri // CHUNK == ci // CHUNK)).astype(BF16)
    h1, h2 = _split2(da)
    acs = _dot(tril, h1) + _dot(tril, h2)
    sel = sel_ref[...]
    dte_ref[...] = _dot2(dt, sel)
    acse_ref[...] = _dot2(acs, sel)

    li = lax.broadcasted_iota(jnp.int32, (CHUNK, gw), 0)
    si = lax.broadcasted_iota(jnp.int32, (CHUNK, gw), 1) % CHUNK
    diag = li == si
    causal = li >= si
    r2 = lax.broadcasted_iota(jnp.int32, (2 * CHUNK, 2 * CHUNK), 0) // CHUNK
    c2 = lax.broadcasted_iota(jnp.int32, (2 * CHUNK, 2 * CHUNK), 1) // CHUNK
    blockdiag = r2 == c2
    dskip = dskip_ref[...]
    nw = nw_ref[...]

    def chunk(c, carry):
        rows = pl.ds(pl.multiple_of(c * CHUNK, CHUNK), CHUNK)
        cc = c_ref[rows, :]
        bc = b_ref[rows, :]
        xs = xs_ref[rows, :].astype(F32)
        dte = dte_ref[rows, :]
        acse = acse_ref[rows, :]
        xdt = xs * dte
        acs_last = acse[CHUNK - 1:CHUNK, :]
        acs_t = jnp.sum(jnp.where(diag, acse, 0.0), axis=0, keepdims=True)
        decay = jnp.exp2(jnp.where(causal, acse - acs_t, -jnp.inf))
        bdup = jnp.concatenate([bc, bc], axis=0)
        cb2 = lax.dot_general(cc, bdup, (((1,), (1,)), ((), ())), preferred_element_type=F32)
        xdt16 = xdt.astype(BF16)
        ys = []
        for q in range(gw // (2 * CHUNK)):
            lanes = slice(q * 2 * CHUNK, (q + 1) * 2 * CHUNK)
            wq = (cb2 * decay[:, lanes]).astype(BF16)
            xq = xdt16[:, lanes]
            bd = jnp.where(blockdiag, jnp.concatenate([xq, xq], axis=0), jnp.zeros((), BF16))
            ys.append(_dot(wq, bd))
        y = jnp.concatenate(ys, axis=1) if len(ys) > 1 else ys[0]
        state = state_ref[...]
        y = y + _dot(cc, state.astype(BF16)) * jnp.exp2(acse)
        xdtw = (xdt * jnp.exp2(acs_last - acse)).astype(BF16)
        state_ref[...] = state * jnp.exp2(acs_last) + lax.dot_general(
            bc, xdtw, (((0,), (0,)), ((), ())), preferred_element_type=F32)
        y = y + dskip * xs
        zc = z_ref[rows, :].astype(F32)
        hh = y * (zc * jax.nn.sigmoid(zc))
        ms = jnp.mean(hh * hh, axis=-1, keepdims=True)
        o_ref[rows, :] = (hh * lax.rsqrt(ms + RMS_EPS) * nw).astype(o_ref.dtype)
        return carry

    lax.fori_loop(0, nchunk, chunk, 0, unroll=2)


def _ssd_scan(zx, xbc, dt_raw, dt_bias, a_log, d_skip, norm_w, batch, seq, d_inner, ts=512):
    t = zx.shape[0]
    nheads = dt_raw.shape[1]
    gw = d_inner // SSD_GROUPS
    hpg = nheads // SSD_GROUPS
    assert gw % (2 * CHUNK) == 0 and gw == hpg * SSD_HEADDIM
    ts = min(ts, seq)
    spb = seq // ts
    nb = SSD_STATE // 128
    head_of_lane = jnp.arange(d_inner, dtype=jnp.int32) // SSD_HEADDIM
    sel = (jnp.arange(nheads, dtype=jnp.int32)[:, None] == head_of_lane[None, :]).astype(BF16)
    dskip = jnp.repeat(d_skip.astype(F32), SSD_HEADDIM).reshape(1, d_inner)
    a = -jnp.exp(a_log.astype(F32)).reshape(1, nheads) * LOG2E
    kern = functools.partial(_ssd_scan_kernel, ts=ts, gw=gw)
    boff = d_inner // SSD_STATE
    return pl.pallas_call(
        kern,
        grid=(batch, SSD_GROUPS, spb),
        in_specs=[
            pl.BlockSpec((ts, gw), lambda b, g, s: (b * spb + s, g)),
            pl.BlockSpec((ts, SSD_STATE), lambda b, g, s: (b * spb + s, boff + g * nb)),
            pl.BlockSpec((ts, SSD_STATE), lambda b, g, s: (b * spb + s, boff + (SSD_GROUPS + g) * nb)),
            pl.BlockSpec((ts, gw), lambda b, g, s: (b * spb + s, g)),
            pl.BlockSpec((ts, nheads), lambda b, g, s: (b * spb + s, 0)),
            pl.BlockSpec((1, nheads), lambda b, g, s: (0, 0)),
            pl.BlockSpec((1, nheads), lambda b, g, s: (0, 0)),
            pl.BlockSpec((nheads, gw), lambda b, g, s: (0, g)),
            pl.BlockSpec((1, gw), lambda b, g, s: (0, g)),
            pl.BlockSpec((1, gw), lambda b, g, s: (0, g)),
        ],
        out_specs=pl.BlockSpec((ts, gw), lambda b, g, s: (b * spb + s, g)),
        out_shape=jax.ShapeDtypeStruct((t, d_inner), BF16),
        scratch_shapes=[pltpu.VMEM((SSD_STATE, gw), F32),
                        pltpu.VMEM((ts, gw), F32),
                        pltpu.VMEM((ts, gw), F32)],
        compiler_params=_cparams("parallel", "parallel", "arbitrary"),
        name="ssd_scan",
    )(xbc, xbc, xbc, zx, dt_raw, dt_bias.reshape(1, nheads).astype(F32), a, sel, dskip,
      norm_w.reshape(1, d_inner).astype(F32))


def _ssd_mixer(x16, in_proj, out_proj16, layer, conv_w, conv_b, dt_bias, a_log, d_skip, norm_w, batch, seq):
    d_inner = out_proj16.shape[1]
    cdim = conv_w.shape[1]
    zx = _matmul_w32(x16, in_proj, layer, BF16, 1024, 512, n=d_inner + cdim)
    dt_raw = _matmul_w32(x16, in_proj, layer, F32, 1024, 128, col0=d_inner + cdim)
    xbc = _ssd_conv(zx, conv_w, conv_b, batch, seq, d_inner)
    h = _ssd_scan(zx, xbc, dt_raw, dt_bias, a_log, d_skip, norm_w, batch, seq, d_inner)
    return _matmul(h, out_proj16, BF16, 1024, 512, w_layer=layer)


def _sc_gate_kernel(gb_ref, gc_ref, h_ref, w_ref, o_ref):
    u = gc_ref[...].astype(F32) * h_ref[...].astype(F32)
    w = w_ref[...]
    kw = w.shape[0]
    acc = u * w[kw - 1:kw, :]
    for k in range(kw - 1):
        acc = acc + _shift_rows(u, kw - 1 - k) * w[k:k + 1, :]
    o_ref[...] = (gb_ref[...].astype(F32) * acc).astype(o_ref.dtype)


def _sc_gate(bch, conv_w, batch, seq, d, tc=128):
    t = bch.shape[0]
    nj = d // tc
    return pl.pallas_call(
        _sc_gate_kernel,
        grid=(batch, nj),
        in_specs=[pl.BlockSpec((seq, tc), lambda b, j: (b, j)),
                  pl.BlockSpec((seq, tc), lambda b, j: (b, nj + j)),
                  pl.BlockSpec((seq, tc), lambda b, j: (b, 2 * nj + j)),
                  pl.BlockSpec((conv_w.shape[0], tc), lambda b, j: (0, j))],
        out_specs=pl.BlockSpec((seq, tc), lambda b, j: (b, j)),
        out_shape=jax.ShapeDtypeStruct((t, d), BF16),
        compiler_params=_cparams("parallel", "parallel"),
        name="shortconv_gate",
    )(bch, bch, bch, conv_w.astype(F32))


def _shortconv_mixer(x16, in_proj, out_proj, layer, conv_w, batch, seq):
    d = x16.shape[1]
    bch = _matmul_w32(x16, in_proj, layer, BF16, 1024, 512)
    y = _sc_gate(bch, conv_w, batch, seq, d)
    return _matmul_w32(y, out_proj, layer, BF16, 1024, 512)


def _top_values(s, k, rows):
    vals = []
    cur = s
    for _ in range(k):
        m = jnp.max(cur, axis=0, keepdims=True)
        vals.append(m)
        cur = jnp.where(cur == m, -jnp.inf, cur)
    if rows > k:
        vals.append(jnp.full((rows - k, s.shape[1]), -jnp.inf, F32))
    return jnp.concatenate(vals, axis=0)


def _peer_topk_kernel(qt_ref, keys_ref, s1c_ref, s2_ref, tauc_ref):
    dh = PEER_HALF_DIM
    k = PEER_TOPK

    def head(h, carry):
        r0 = pl.multiple_of(h * 2 * dh, 2 * dh)
        s1 = _dot(keys_ref[h, 0], qt_ref[pl.ds(r0, dh), :]) * LOG2E
        s2 = _dot(keys_ref[h, 1], qt_ref[pl.ds(r0 + dh, dh), :]) * LOG2E
        sa = _top_values(s1, k + 1, 24)
        sb = _top_values(s2, k + 1, 24)
        pieces = [sa[0:1, :] + sb]
        for r in range(1, 8):
            pieces.append(sa[r:r + 1, :] + sb[0:8, :])
        pieces.append(sa[8:24, :] + sb[0:1, :])
        top = _top_values(jnp.concatenate(pieces, axis=0), k + 1, k + 1)
        mx = top[0:1, :]
        z = jnp.sum(jnp.exp2(top[0:k, :] - mx), axis=0, keepdims=True)
        cst = mx + jnp.log(z) * LOG2E
        tau = 0.5 * (top[k - 1:k, :] + top[k:k + 1, :])
        s1c_ref[h] = s1 - cst
        s2_ref[h] = s2
        tauc_ref[h] = tau - cst
        return carry

    lax.fori_loop(0, PEER_HEADS, head, 0)


def _peer_topk(qt, keys16, tm=256):
    t = qt.shape[1]
    tm = _tile(t, tm)
    sc_shape = jax.ShapeDtypeStruct((PEER_HEADS, PEER_NKEYS, t), F32)
    row_shape = jax.ShapeDtypeStruct((PEER_HEADS, 1, t), F32)
    sc_spec = pl.BlockSpec((PEER_HEADS, PEER_NKEYS, tm), lambda i: (0, 0, i))
    row_spec = pl.BlockSpec((PEER_HEADS, 1, tm), lambda i: (0, 0, i))
    return pl.pallas_call(
        _peer_topk_kernel,
        grid=(t // tm,),
        in_specs=[pl.BlockSpec((qt.shape[0], tm), lambda i: (0, i)),
                  pl.BlockSpec(keys16.shape, lambda i: (0, 0, 0, 0))],
        out_specs=[sc_spec, sc_spec, row_spec],
        out_shape=[sc_shape, sc_shape, row_shape],
        compiler_params=_cparams("parallel"),
        name="peer_topk",
    )(qt, keys16)


def _peer_dense_kernel(xt_ref, u_ref, vt_ref, s1c_ref, s2_ref, tauc_ref, o_ref,
                       h_ref, act_ref, s1b_ref, taub_ref, *, te, ne):
    g = pl.program_id(0)
    e1 = jnp.maximum(g - 1, 0) % ne
    e2 = jnp.maximum(g - 2, 0) % ne
    new = g % 2
    old = 1 - new
    nk = PEER_NKEYS
    ni = te // nk
    d, tm = o_ref.shape
    gp = s1b_ref.shape[0]
    db = d // (ni // gp)
    rb = 2 * SUBLANES

    @pl.when(g == 0)
    def _():
        h_ref[...] = jnp.zeros_like(h_ref)
        act_ref[...] = jnp.zeros_like(act_ref)

    @pl.when(e2 == 0)
    def _():
        o_ref[...] = jnp.zeros_like(o_ref)

    @pl.when(e1 == 0)
    def _():
        for h in range(PEER_HEADS):
            taub_ref[h] = jnp.broadcast_to(tauc_ref[h], (SUBLANES, tm))

    h_ref[new] = _dot(u_ref[...], xt_ref[...])

    def piece(k, carry):
        rows = pl.ds(pl.multiple_of(k * db, db), db)
        o_ref[rows, :] += _dot(vt_ref[rows, :], act_ref[old])
        for q in range(gp):
            kq = k * gp + q
            for h in range(PEER_HEADS):
                s1b_ref[q, h] = jnp.broadcast_to(s1c_ref[h, pl.ds(e1 * ni + kq, 1), :], (SUBLANES, tm))
            for jb in range(nk // rb):
                halves = []
                for j0 in (jb * rb, jb * rb + SUBLANES):
                    gate = None
                    for h in range(PEER_HEADS):
                        a = s1b_ref[q, h] + s2_ref[h, j0:j0 + SUBLANES, :]
                        gh = jnp.where(a >= taub_ref[h], jnp.exp2(a), 0.0)
                        gate = gh if gate is None else gate + gh
                    halves.append(gate)
                r0 = pl.multiple_of(kq * nk, nk) + jb * rb
                pre = h_ref[old, pl.ds(r0, rb), :]
                act = 0.5 * pre * (1.0 + lax.erf(pre * (1.0 / math.sqrt(2.0))))
                act_ref[new, pl.ds(r0, rb), :] = (act * jnp.concatenate(halves, axis=0)).astype(BF16)
        return carry

    lax.fori_loop(0, ni // gp, piece, 0)


def _peer_dense(xt, u16, vt16, layer, s1c, s2, tauc, tm=512, te=512, groups_per_piece=1):
    d, t = xt.shape
    nexp = u16.shape[1]
    tm, te = _tile(t, tm), _tile(nexp, te)
    assert te % (PEER_NKEYS * groups_per_piece) == 0 and d % (te // PEER_NKEYS) == 0
    ne = nexp // te
    npairs = (t // tm) * ne
    pair = lambda g, lag: jnp.clip(g - lag, 0, npairs - 1)
    sc_spec = pl.BlockSpec((PEER_HEADS, PEER_NKEYS, tm), lambda g: (0, 0, pair(g, 1) // ne))
    row_spec = pl.BlockSpec((PEER_HEADS, 1, tm), lambda g: (0, 0, pair(g, 1) // ne))
    return pl.pallas_call(
        functools.partial(_peer_dense_kernel, te=te, ne=ne),
        grid=(npairs + 2,),
        in_specs=[pl.BlockSpec((d, tm), lambda g: (0, pair(g, 0) // ne)),
                  pl.BlockSpec((None, te, d), lambda g: (layer, pair(g, 0) % ne, 0)),
                  pl.BlockSpec((None, d, te), lambda g: (layer, 0, pair(g, 2) % ne)),
                  sc_spec, sc_spec, row_spec],
        out_specs=pl.BlockSpec((d, tm), lambda g: (0, pair(g, 2) // ne), pipeline_mode=pl.Buffered(1)),
        out_shape=jax.ShapeDtypeStruct((d, t), F32),
        scratch_shapes=[pltpu.VMEM((2, te, tm), F32),
                        pltpu.VMEM((2, te, tm), BF16),
                        pltpu.VMEM((groups_per_piece, PEER_HEADS, SUBLANES, tm), F32),
                        pltpu.VMEM((PEER_HEADS, SUBLANES, tm), F32)],
        compiler_params=_cparams("arbitrary"),
        name="peer_dense",
    )(xt, u16, vt16, s1c, s2, tauc)


def _peer_ffn(xt, wqt16, subkeys, u16, vt16, layer):
    qt = _matmul(wqt16, xt, BF16, 1024, 1024, a_layer=layer)
    s1c, s2, tauc = _peer_topk(qt, subkeys.astype(BF16))
    return _peer_dense(xt, u16, vt16, layer, s1c, s2, tauc)


def kernel(x, ssd_in_proj, ssd_conv_w, ssd_conv_b, ssd_dt_bias, ssd_A_log, ssd_D, ssd_norm_w,
           ssd_out_proj, sc_in_proj, sc_conv_w, sc_out_proj, peer_wq, peer_subkeys, peer_u,
           peer_v, ln1_g, ln1_b, ln2_g, ln2_b):
    batch, seq, d = x.shape
    depth = peer_wq.shape[0]
    alpha = (2 * depth) ** 0.25
    x32 = x.reshape(batch * seq, d).astype(F32)
    x16 = x32.astype(BF16)
    ssd_out16 = ssd_out_proj.astype(BF16)
    wqt16 = jnp.swapaxes(peer_wq, 1, 2).astype(BF16)
    u16 = peer_u.astype(BF16)
    vt16 = jnp.swapaxes(peer_v, 1, 2).astype(BF16)
    for i in range(depth):
        j = i // 2
        if i % 2 == 0:
            mix = _ssd_mixer(x16, ssd_in_proj, ssd_out16, j, ssd_conv_w[j], ssd_conv_b[j], ssd_dt_bias[j],
                             ssd_A_log[j], ssd_D[j], ssd_norm_w[j], batch, seq)
        else:
            mix = _shortconv_mixer(x16, sc_in_proj, sc_out_proj, j, sc_conv_w[j], batch, seq)
        x32, xt16 = _residual_ln(x32, mix, ln1_g[i], ln1_b[i], alpha, transposed=False)
        ffn_t = _peer_ffn(xt16, wqt16, peer_subkeys[i], u16, vt16, i)
        x32, x16 = _residual_ln(x32, ffn_t, ln2_g[i], ln2_b[i], alpha, transposed=True)
    return x32.reshape(batch, seq, d).astype(x.dtype)
```

```python
import functools
import math

import jax
import jax.numpy as jnp
from jax import lax
from jax.experimental import pallas as pl
from jax.experimental.pallas import tpu as pltpu

F32 = jnp.float32
BF16 = jnp.bfloat16

CHUNK = 64
SSD_HEADDIM = 64
SSD_GROUPS = 8
SSD_STATE = 128
PEER_HEADS = 8
PEER_NKEYS = 128
PEER_TOPK = 16
PEER_HALF_DIM = 128
LN_EPS = 1e-5
RMS_EPS = 1e-5
LOG2E = 1.4426950408889634

SUBLANES = 8
VMEM_LIMIT_BYTES = 56 * 1024 * 1024


def _cparams(*sem):
    return pltpu.CompilerParams(dimension_semantics=sem, vmem_limit_bytes=VMEM_LIMIT_BYTES)


def _tile(n, pref):
    t = min(pref, n)
    while n % t:
        t //= 2
    return t


def _split2(x):
    hi = x.astype(BF16)
    lo = (x - hi.astype(F32)).astype(BF16)
    return hi, lo


def _dot(a, b):
    return jnp.dot(a, b, preferred_element_type=F32)


def _dot2(x, sel):
    hi, lo = _split2(x)
    return _dot(hi, sel) + _dot(lo, sel)


def _mm_kernel(a_ref, w_ref, o_ref):
    o_ref[...] = _dot(a_ref[...], w_ref[...]).astype(o_ref.dtype)


def _matmul(a, w, out_dtype, tm, tn, a_layer=None, w_layer=None):
    m, k = a.shape[-2:]
    n = w.shape[-1]
    tm, tn = _tile(m, tm), _tile(n, tn)
    a_spec = (pl.BlockSpec((tm, k), lambda i, j: (i, 0)) if a_layer is None else
              pl.BlockSpec((None, tm, k), lambda i, j: (a_layer, i, 0)))
    w_spec = (pl.BlockSpec((k, tn), lambda i, j: (0, j)) if w_layer is None else
              pl.BlockSpec((None, k, tn), lambda i, j: (w_layer, 0, j)))
    return pl.pallas_call(
        _mm_kernel,
        grid=(m // tm, n // tn),
        in_specs=[a_spec, w_spec],
        out_specs=pl.BlockSpec((tm, tn), lambda i, j: (i, j)),
        out_shape=jax.ShapeDtypeStruct((m, n), out_dtype),
        compiler_params=_cparams("parallel", "arbitrary"),
        name="matmul",
    )(a, w)


def _mm_wcast_kernel(a_ref, w_ref, o_ref, w16_ref):
    @pl.when(pl.program_id(1) == 0)
    def _():
        w16_ref[...] = w_ref[...].astype(BF16)

    o_ref[...] = _dot(a_ref[...], w16_ref[...]).astype(o_ref.dtype)


def _matmul_w32(a, w32, layer, out_dtype, tm, tn, col0=0, n=None):
    m, k = a.shape
    n = w32.shape[2] - col0 if n is None else n
    tm, tn = _tile(m, tm), _tile(n, tn)
    assert col0 % tn == 0
    j0 = col0 // tn
    return pl.pallas_call(
        _mm_wcast_kernel,
        grid=(n // tn, m // tm),
        in_specs=[pl.BlockSpec((tm, k), lambda j, i: (i, 0)),
                  pl.BlockSpec((None, k, tn), lambda j, i: (layer, 0, j0 + j))],
        out_specs=pl.BlockSpec((tm, tn), lambda j, i: (i, j)),
        out_shape=jax.ShapeDtypeStruct((m, n), out_dtype),
        scratch_shapes=[pltpu.VMEM((k, tn), BF16)],
        compiler_params=_cparams("parallel", "arbitrary"),
        name="matmul_w32",
    )(a, w32)


def _ln_kernel(x_ref, y_ref, g_ref, b_ref, o32_ref, o16_ref, *, alpha, transposed):
    y = y_ref[...].astype(F32)
    if transposed:
        y = y.T
    h = alpha * x_ref[...] + y
    mu = jnp.mean(h, axis=-1, keepdims=True)
    d = h - mu
    var = jnp.mean(d * d, axis=-1, keepdims=True)
    out = d * lax.rsqrt(var + LN_EPS) * g_ref[...] + b_ref[...]
    o32_ref[...] = out
    if transposed:
        o16_ref[...] = out.astype(BF16)
    else:
        o16_ref[...] = out.T.astype(BF16)


def _residual_ln(x, y, g, b, alpha, transposed, tm=256):
    t, d = x.shape
    tm = _tile(t, tm)
    row = pl.BlockSpec((tm, d), lambda i: (i, 0))
    col = pl.BlockSpec((d, tm), lambda i: (0, i))
    vec = pl.BlockSpec((1, d), lambda i: (0, 0))
    return pl.pallas_call(
        functools.partial(_ln_kernel, alpha=alpha, transposed=transposed),
        grid=(t // tm,),
        in_specs=[row, col if transposed else row, vec, vec],
        out_specs=[row, row if transposed else col],
        out_shape=[jax.ShapeDtypeStruct((t, d), F32),
                   jax.ShapeDtypeStruct((t, d) if transposed else (d, t), BF16)],
        compiler_params=_cparams("parallel"),
        name="residual_ln",
    )(x, y, g.reshape(1, d).astype(F32), b.reshape(1, d).astype(F32))


def _shift_rows(h, k):
    rolled = pltpu.roll(h, k, 0)
    rows = lax.broadcasted_iota(jnp.int32, h.shape, 0)
    return jnp.where(rows >= k, rolled, 0.0)


def _ssd_conv_kernel(x_ref, w_ref, b_ref, o_ref):
    h = x_ref[...].astype(F32)
    w = w_ref[...]
    kw = w.shape[0]
    acc = h * w[kw - 1:kw, :] + b_ref[...]
    for k in range(kw - 1):
        acc = acc + _shift_rows(h, kw - 1 - k) * w[k:k + 1, :]
    o_ref[...] = (acc * jax.nn.sigmoid(acc)).astype(o_ref.dtype)


def _ssd_conv(zx, conv_w, conv_b, batch, seq, d_inner, tc=128):
    t = zx.shape[0]
    cdim = conv_w.shape[1]
    off = d_inner // tc
    return pl.pallas_call(
        _ssd_conv_kernel,
        grid=(batch, cdim // tc),
        in_specs=[pl.BlockSpec((seq, tc), lambda b, j: (b, off + j)),
                  pl.BlockSpec((conv_w.shape[0], tc), lambda b, j: (0, j)),
                  pl.BlockSpec((1, tc), lambda b, j: (0, j))],
        out_specs=pl.BlockSpec((seq, tc), lambda b, j: (b, j)),
        out_shape=jax.ShapeDtypeStruct((t, cdim), BF16),
        compiler_params=_cparams("parallel", "parallel"),
        name="ssd_conv_silu",
    )(zx, conv_w.astype(F32), conv_b.reshape(1, cdim).astype(F32))


def _ssd_scan_kernel(xs_ref, b_ref, c_ref, z_ref, dtraw_ref, dtb_ref, a_ref, sel_ref,
                     dskip_ref, nw_ref, o_ref, state_ref, dte_ref, acse_ref, *, ts, gw):
    nchunk = ts // CHUNK

    @pl.when(pl.program_id(2) == 0)
    def _():
        state_ref[...] = jnp.zeros_like(state_ref)

    xr = dtraw_ref[...] + dtb_ref[...]
    dt = jnp.maximum(xr, 0.0) + jnp.log1p(jnp.exp(-jnp.abs(xr)))
    da = dt * a_ref[...]
    ri = lax.broadcasted_iota(jnp.int32, (ts, ts), 0)
    ci = lax.broadcasted_iota(jnp.int32, (ts, ts), 1)
    tril = ((ri >= ci) & (ri // CHUNK == ci // CHUNK)).astype(BF16)
    h1, h2 = _split2(da)
    acs = _dot(tril, h1) + _dot(tril, h2)
    sel = sel_ref[...]
    dte_ref[...] = _dot2(dt, sel)
    acse_ref[...] = _dot2(acs, sel)

    li = lax.broadcasted_iota(jnp.int32, (CHUNK, gw), 0)
    si = lax.broadcasted_iota(jnp.int32, (CHUNK, gw), 1) % CHUNK
    diag = li == si
    causal = li >= si
    r2 = lax.broadcasted_iota(jnp.int32, (2 * CHUNK, 2 * CHUNK), 0) // CHUNK
    c2 = lax.broadcasted_iota(jnp.int32, (2 * CHUNK, 2 * CHUNK), 1) // CHUNK
    blockdiag = r2 == c2
    dskip = dskip_ref[...]
    nw = nw_ref[...]

    def chunk(c, carry):
        rows = pl.ds(pl.multiple_of(c * CHUNK, CHUNK), CHUNK)
        cc = c_ref[rows, :]
        bc = b_ref[rows, :]
        xs = xs_ref[rows, :].astype(F32)
        dte = dte_ref[rows, :]
        acse = acse_ref[rows, :]
        xdt = xs * dte
        acs_last = acse[CHUNK - 1:CHUNK, :]
        acs_t = jnp.sum(jnp.where(diag, acse, 0.0), axis=0, keepdims=True)
        decay = jnp.exp2(jnp.where(causal, acse - acs_t, -jnp.inf))
        bdup = jnp.concatenate([bc, bc], axis=0)
        cb2 = lax.dot_general(cc, bdup, (((1,), (1,)), ((), ())), preferred_element_type=F32)
        xdt16 = xdt.astype(BF16)
        ys = []
        for q in range(gw // (2 * CHUNK)):
            lanes = slice(q * 2 * CHUNK, (q + 1) * 2 * CHUNK)
            wq = (cb2 * decay[:, lanes]).astype(BF16)
            xq = xdt16[:, lanes]
            bd = jnp.where(blockdiag, jnp.concatenate([xq, xq], axis=0), jnp.zeros((), BF16))
            ys.append(_dot(wq, bd))
        y = jnp.concatenate(ys, axis=1) if len(ys) > 1 else ys[0]
        state = state_ref[...]
        y = y + _dot(cc, state.astype(BF16)) * jnp.exp2(acse)
        xdtw = (xdt * jnp.exp2(acs_last - acse)).astype(BF16)
        state_ref[...] = state * jnp.exp2(acs_last) + lax.dot_general(
            bc, xdtw, (((0,), (0,)), ((), ())), preferred_element_type=F32)
        y = y + dskip * xs
        zc = z_ref[rows, :].astype(F32)
        hh = y * (zc * jax.nn.sigmoid(zc))
        ms = jnp.mean(hh * hh, axis=-1, keepdims=True)
        o_ref[rows, :] = (hh * lax.rsqrt(ms + RMS_EPS) * nw).astype(o_ref.dtype)
        return carry

    lax.fori_loop(0, nchunk, chunk, 0, unroll=2)


def _ssd_scan(zx, xbc, dt_raw, dt_bias, a_log, d_skip, norm_w, batch, seq, d_inner, ts=512):
    t = zx.shape[0]
    nheads = dt_raw.shape[1]
    gw = d_inner // SSD_GROUPS
    hpg = nheads // SSD_GROUPS
    assert gw % (2 * CHUNK) == 0 and gw == hpg * SSD_HEADDIM
    ts = min(ts, seq)
    spb = seq // ts
    nb = SSD_STATE // 128
    head_of_lane = jnp.arange(d_inner, dtype=jnp.int32) // SSD_HEADDIM
    sel = (jnp.arange(nheads, dtype=jnp.int32)[:, None] == head_of_lane[None, :]).astype(BF16)
    dskip = jnp.repeat(d_skip.astype(F32), SSD_HEADDIM).reshape(1, d_inner)
    a = -jnp.exp(a_log.astype(F32)).reshape(1, nheads) * LOG2E
    kern = functools.partial(_ssd_scan_kernel, ts=ts, gw=gw)
    boff = d_inner // SSD_STATE
    return pl.pallas_call(
        kern,
        grid=(batch, SSD_GROUPS, spb),
        in_specs=[
            pl.BlockSpec((ts, gw), lambda b, g, s: (b * spb + s, g)),
            pl.BlockSpec((ts, SSD_STATE), lambda b, g, s: (b * spb + s, boff + g * nb)),
            pl.BlockSpec((ts, SSD_STATE), lambda b, g, s: (b * spb + s, boff + (SSD_GROUPS + g) * nb)),
            pl.BlockSpec((ts, gw), lambda b, g, s: (b * spb + s, g)),
            pl.BlockSpec((ts, nheads), lambda b, g, s: (b * spb + s, 0)),
            pl.BlockSpec((1, nheads), lambda b, g, s: (0, 0)),
            pl.BlockSpec((1, nheads), lambda b, g, s: (0, 0)),
            pl.BlockSpec((nheads, gw), lambda b, g, s: (0, g)),
            pl.BlockSpec((1, gw), lambda b, g, s: (0, g)),
            pl.BlockSpec((1, gw), lambda b, g, s: (0, g)),
        ],
        out_specs=pl.BlockSpec((ts, gw), lambda b, g, s: (b * spb + s, g)),
        out_shape=jax.ShapeDtypeStruct((t, d_inner), BF16),
        scratch_shapes=[pltpu.VMEM((SSD_STATE, gw), F32),
                        pltpu.VMEM((ts, gw), F32),
                        pltpu.VMEM((ts, gw), F32)],
        compiler_params=_cparams("parallel", "parallel", "arbitrary"),
        name="ssd_scan",
    )(xbc, xbc, xbc, zx, dt_raw, dt_bias.reshape(1, nheads).astype(F32), a, sel, dskip,
      norm_w.reshape(1, d_inner).astype(F32))


def _ssd_mixer(x16, in_proj, out_proj16, layer, conv_w, conv_b, dt_bias, a_log, d_skip, norm_w, batch, seq):
    d_inner = out_proj16.shape[1]
    cdim = conv_w.shape[1]
    zx = _matmul_w32(x16, in_proj, layer, BF16, 1024, 512, n=d_inner + cdim)
    dt_raw = _matmul_w32(x16, in_proj, layer, F32, 1024, 128, col0=d_inner + cdim)
    xbc = _ssd_conv(zx, conv_w, conv_b, batch, seq, d_inner)
    h = _ssd_scan(zx, xbc, dt_raw, dt_bias, a_log, d_skip, norm_w, batch, seq, d_inner)
    return _matmul(h, out_proj16, BF16, 1024, 512, w_layer=layer)


def _sc_gate_kernel(gb_ref, gc_ref, h_ref, w_ref, o_ref):
    u = gc_ref[...].astype(F32) * h_ref[...].astype(F32)
    w = w_ref[...]
    kw = w.shape[0]
    acc = u * w[kw - 1:kw, :]
    for k in range(kw - 1):
        acc = acc + _shift_rows(u, kw - 1 - k) * w[k:k + 1, :]
    o_ref[...] = (gb_ref[...].astype(F32) * acc).astype(o_ref.dtype)


def _sc_gate(bch, conv_w, batch, seq, d, tc=128):
    t = bch.shape[0]
    nj = d // tc
    return pl.pallas_call(
        _sc_gate_kernel,
        grid=(batch, nj),
        in_specs=[pl.BlockSpec((seq, tc), lambda b, j: (b, j)),
                  pl.BlockSpec((seq, tc), lambda b, j: (b, nj + j)),
                  pl.BlockSpec((seq, tc), lambda b, j: (b, 2 * nj + j)),
                  pl.BlockSpec((conv_w.shape[0], tc), lambda b, j: (0, j))],
        out_specs=pl.BlockSpec((seq, tc), lambda b, j: (b, j)),
        out_shape=jax.ShapeDtypeStruct((t, d), BF16),
        compiler_params=_cparams("parallel", "parallel"),
        name="shortconv_gate",
    )(bch, bch, bch, conv_w.astype(F32))


def _shortconv_mixer(x16, in_proj, out_proj, layer, conv_w, batch, seq):
    d = x16.shape[1]
    bch = _matmul_w32(x16, in_proj, layer, BF16, 1024, 512)
    y = _sc_gate(bch, conv_w, batch, seq, d)
    return _matmul_w32(y, out_proj, layer, BF16, 1024, 512)


def _merge_exchange_network(n):
    t = max(1, (n - 1).bit_length())
    pairs = []
    p = 1 << (t - 1)
    while p > 0:
        q, r, d = 1 << (t - 1), 0, p
        while d > 0:
            pairs += [(i, i + d) for i in range(n - d) if (i & p) == r]
            d, q, r = q - p, q // 2, p
        p //= 2
    return pairs


def _top_values(s, k, rows):
    nb = s.shape[0] // SUBLANES
    blocks = [s[b * SUBLANES:(b + 1) * SUBLANES, :] for b in range(nb)]
    for i, j in _merge_exchange_network(nb):
        blocks[i], blocks[j] = jnp.maximum(blocks[i], blocks[j]), jnp.minimum(blocks[i], blocks[j])
    vals = []
    for p in range(k):
        m = jnp.max(blocks[0], axis=0, keepdims=True)
        vals.append(m)
        popped = blocks[0] == m
        for q in range(min(nb, k - 1 - p)):
            below = blocks[q + 1] if q + 1 < nb else -jnp.inf
            blocks[q] = jnp.where(popped, below, blocks[q])
    if rows > k:
        vals.append(jnp.full((rows - k, s.shape[1]), -jnp.inf, F32))
    return jnp.concatenate(vals, axis=0)


def _peer_topk_kernel(qt_ref, keys_ref, s1c_ref, s2_ref, tauc_ref):
    dh = PEER_HALF_DIM
    k = PEER_TOPK

    def head(h, carry):
        r0 = pl.multiple_of(h * 2 * dh, 2 * dh)
        s1 = _dot(keys_ref[h, 0], qt_ref[pl.ds(r0, dh), :]) * LOG2E
        s2 = _dot(keys_ref[h, 1], qt_ref[pl.ds(r0 + dh, dh), :]) * LOG2E
        sa = _top_values(s1, k + 1, 24)
        sb = _top_values(s2, k + 1, 24)
        pieces = [sa[0:1, :] + sb]
        for r in range(1, 8):
            pieces.append(sa[r:r + 1, :] + sb[0:8, :])
        pieces.append(sa[8:24, :] + sb[0:1, :])
        top = _top_values(jnp.concatenate(pieces, axis=0), k + 1, k + 1)
        mx = top[0:1, :]
        z = jnp.sum(jnp.exp2(top[0:k, :] - mx), axis=0, keepdims=True)
        cst = mx + jnp.log(z) * LOG2E
        tau = 0.5 * (top[k - 1:k, :] + top[k:k + 1, :])
        s1c_ref[h] = s1 - cst
        s2_ref[h] = s2
        tauc_ref[h] = tau - cst
        return carry

    lax.fori_loop(0, PEER_HEADS, head, 0)


def _peer_topk(qt, keys16, tm=512):
    t = qt.shape[1]
    tm = _tile(t, tm)
    sc_shape = jax.ShapeDtypeStruct((PEER_HEADS, PEER_NKEYS, t), F32)
    row_shape = jax.ShapeDtypeStruct((PEER_HEADS, 1, t), F32)
    sc_spec = pl.BlockSpec((PEER_HEADS, PEER_NKEYS, tm), lambda i: (0, 0, i))
    row_spec = pl.BlockSpec((PEER_HEADS, 1, tm), lambda i: (0, 0, i))
    return pl.pallas_call(
        _peer_topk_kernel,
        grid=(t // tm,),
        in_specs=[pl.BlockSpec((qt.shape[0], tm), lambda i: (0, i)),
                  pl.BlockSpec(keys16.shape, lambda i: (0, 0, 0, 0))],
        out_specs=[sc_spec, sc_spec, row_spec],
        out_shape=[sc_shape, sc_shape, row_shape],
        compiler_params=_cparams("parallel"),
        name="peer_topk",
    )(qt, keys16)


def _peer_dense_kernel(xt_ref, u_ref, vt_ref, s1c_ref, s2_ref, tauc_ref, o_ref,
                       h_ref, act_ref, s1b_ref, taub_ref, *, te, ne):
    g = pl.program_id(0)
    e1 = jnp.maximum(g - 1, 0) % ne
    e2 = jnp.maximum(g - 2, 0) % ne
    new = g % 2
    old = 1 - new
    nk = PEER_NKEYS
    ni = te // nk
    d, tm = o_ref.shape
    gp = s1b_ref.shape[0]
    db = d // (ni // gp)
    rb = 2 * SUBLANES

    @pl.when(g == 0)
    def _():
        h_ref[...] = jnp.zeros_like(h_ref)
        act_ref[...] = jnp.zeros_like(act_ref)

    @pl.when(e2 == 0)
    def _():
        o_ref[...] = jnp.zeros_like(o_ref)

    @pl.when(e1 == 0)
    def _():
        for h in range(PEER_HEADS):
            taub_ref[h] = jnp.broadcast_to(tauc_ref[h], (SUBLANES, tm))

    h_ref[new] = _dot(u_ref[...], xt_ref[...])

    def piece(k, carry):
        rows = pl.ds(pl.multiple_of(k * db, db), db)
        o_ref[rows, :] += _dot(vt_ref[rows, :], act_ref[old])
        for q in range(gp):
            kq = k * gp + q
            for h in range(PEER_HEADS):
                s1b_ref[q, h] = jnp.broadcast_to(s1c_ref[h, pl.ds(e1 * ni + kq, 1), :], (SUBLANES, tm))
            for jb in range(nk // rb):
                halves = []
                for j0 in (jb * rb, jb * rb + SUBLANES):
                    gate = None
                    for h in range(PEER_HEADS):
                        a = s1b_ref[q, h] + s2_ref[h, j0:j0 + SUBLANES, :]
                        gh = jnp.where(a >= taub_ref[h], jnp.exp2(a), 0.0)
                        gate = gh if gate is None else gate + gh
                    halves.append(gate)
                r0 = pl.multiple_of(kq * nk, nk) + jb * rb
                pre = h_ref[old, pl.ds(r0, rb), :]
                act = 0.5 * pre * (1.0 + lax.erf(pre * (1.0 / math.sqrt(2.0))))
                act_ref[new, pl.ds(r0, rb), :] = (act * jnp.concatenate(halves, axis=0)).astype(BF16)
        return carry

    lax.fori_loop(0, ni // gp, piece, 0)


def _peer_dense(xt, u16, vt16, layer, s1c, s2, tauc, tm=512, te=512, groups_per_piece=1):
    d, t = xt.shape
    nexp = u16.shape[1]
    tm, te = _tile(t, tm), _tile(nexp, te)
    assert te % (PEER_NKEYS * groups_per_piece) == 0 and d % (te // PEER_NKEYS) == 0
    ne = nexp // te
    npairs = (t // tm) * ne
    pair = lambda g, lag: jnp.clip(g - lag, 0, npairs - 1)
    sc_spec = pl.BlockSpec((PEER_HEADS, PEER_NKEYS, tm), lambda g: (0, 0, pair(g, 1) // ne))
    row_spec = pl.BlockSpec((PEER_HEADS, 1, tm), lambda g: (0, 0, pair(g, 1) // ne))
    return pl.pallas_call(
        functools.partial(_peer_dense_kernel, te=te, ne=ne),
        grid=(npairs + 2,),
        in_specs=[pl.BlockSpec((d, tm), lambda g: (0, pair(g, 0) // ne)),
                  pl.BlockSpec((None, te, d), lambda g: (layer, pair(g, 0) % ne, 0)),
                  pl.BlockSpec((None, d, te), lambda g: (layer, 0, pair(g, 2) % ne)),
                  sc_spec, sc_spec, row_spec],
        out_specs=pl.BlockSpec((d, tm), lambda g: (0, pair(g, 2) // ne), pipeline_mode=pl.Buffered(1)),
        out_shape=jax.ShapeDtypeStruct((d, t), F32),
        scratch_shapes=[pltpu.VMEM((2, te, tm), F32),
                        pltpu.VMEM((2, te, tm), BF16),
                        pltpu.VMEM((groups_per_piece, PEER_HEADS, SUBLANES, tm), F32),
                        pltpu.VMEM((PEER_HEADS, SUBLANES, tm), F32)],
        compiler_params=_cparams("arbitrary"),
        name="peer_dense",
    )(xt, u16, vt16, s1c, s2, tauc)


def _peer_ffn(xt, wqt16, subkeys, u16, vt16, layer):
    qt = _matmul(wqt16, xt, BF16, 1024, 1024, a_layer=layer)
    s1c, s2, tauc = _peer_topk(qt, subkeys.astype(BF16))
    return _peer_dense(xt, u16, vt16, layer, s1c, s2, tauc)


def kernel(x, ssd_in_proj, ssd_conv_w, ssd_conv_b, ssd_dt_bias, ssd_A_log, ssd_D, ssd_norm_w,
           ssd_out_proj, sc_in_proj, sc_conv_w, sc_out_proj, peer_wq, peer_subkeys, peer_u,
           peer_v, ln1_g, ln1_b, ln2_g, ln2_b):
    batch, seq, d = x.shape
    depth = peer_wq.shape[0]
    alpha = (2 * depth) ** 0.25
    x32 = x.reshape(batch * seq, d).astype(F32)
    x16 = x32.astype(BF16)
    ssd_out16 = ssd_out_proj.astype(BF16)
    wqt16 = jnp.swapaxes(peer_wq, 1, 2).astype(BF16)
    u16 = peer_u.astype(BF16)
    vt16 = jnp.swapaxes(peer_v, 1, 2).astype(BF16)
    for i in range(depth):
        j = i // 2
        if i % 2 == 0:
            mix = _ssd_mixer(x16, ssd_in_proj, ssd_out16, j, ssd_conv_w[j], ssd_conv_b[j], ssd_dt_bias[j],
                             ssd_A_log[j], ssd_D[j], ssd_norm_w[j], batch, seq)
        else:
            mix = _shortconv_mixer(x16, sc_in_proj, sc_out_proj, j, sc_conv_w[j], batch, seq)
        x32, xt16 = _residual_ln(x32, mix, ln1_g[i], ln1_b[i], alpha, transposed=False)
        ffn_t = _peer_ffn(xt16, wqt16, peer_subkeys[i], u16, vt16, i)
        x32, x16 = _residual_ln(x32, ffn_t, ln2_g[i], ln2_b[i], alpha, transposed=True)
    return x32.reshape(batch, seq, d).astype(x.dtype)
```

```python
import functools
import math

import jax
import jax.numpy as jnp
from jax import lax
from jax.experimental import pallas as pl
from jax.experimental.pallas import tpu as pltpu

F32 = jnp.float32
BF16 = jnp.bfloat16

CHUNK = 64
SSD_HEADDIM = 64
SSD_GROUPS = 8
SSD_STATE = 128
PEER_HEADS = 8
PEER_NKEYS = 128
PEER_TOPK = 16
PEER_HALF_DIM = 128
LN_EPS = 1e-5
RMS_EPS = 1e-5
LOG2E = 1.4426950408889634

SUBLANES = 8
VMEM_LIMIT_BYTES = 56 * 1024 * 1024


def _cparams(*sem):
    return pltpu.CompilerParams(dimension_semantics=sem, vmem_limit_bytes=VMEM_LIMIT_BYTES)


def _tile(n, pref):
    t = min(pref, n)
    while n % t:
        t //= 2
    return t


def _split2(x):
    hi = x.astype(BF16)
    lo = (x - hi.astype(F32)).astype(BF16)
    return hi, lo


def _dot(a, b):
    return jnp.dot(a, b, preferred_element_type=F32)


def _dot2(x, sel2):
    hi, lo = _split2(x)
    return _dot(jnp.concatenate([hi, lo], axis=1), sel2)


def _mm_kernel(a_ref, w_ref, o_ref):
    o_ref[...] = _dot(a_ref[...], w_ref[...]).astype(o_ref.dtype)


def _matmul(a, w, out_dtype, tm, tn, a_layer=None, w_layer=None):
    m, k = a.shape[-2:]
    n = w.shape[-1]
    tm, tn = _tile(m, tm), _tile(n, tn)
    a_spec = (pl.BlockSpec((tm, k), lambda i, j: (i, 0)) if a_layer is None else
              pl.BlockSpec((None, tm, k), lambda i, j: (a_layer, i, 0)))
    w_spec = (pl.BlockSpec((k, tn), lambda i, j: (0, j)) if w_layer is None else
              pl.BlockSpec((None, k, tn), lambda i, j: (w_layer, 0, j)))
    return pl.pallas_call(
        _mm_kernel,
        grid=(m // tm, n // tn),
        in_specs=[a_spec, w_spec],
        out_specs=pl.BlockSpec((tm, tn), lambda i, j: (i, j)),
        out_shape=jax.ShapeDtypeStruct((m, n), out_dtype),
        compiler_params=_cparams("parallel", "arbitrary"),
        name="matmul",
    )(a, w)


def _mm_wcast_kernel(a_ref, w_ref, o_ref, w16_ref):
    @pl.when(pl.program_id(1) == 0)
    def _():
        w16_ref[...] = w_ref[...].astype(BF16)

    o_ref[...] = _dot(a_ref[...], w16_ref[...]).astype(o_ref.dtype)


def _matmul_w32(a, w32, layer, out_dtype, tm, tn, col0=0, n=None):
    m, k = a.shape
    n = w32.shape[2] - col0 if n is None else n
    tm, tn = _tile(m, tm), _tile(n, tn)
    assert col0 % tn == 0
    j0 = col0 // tn
    return pl.pallas_call(
        _mm_wcast_kernel,
        grid=(n // tn, m // tm),
        in_specs=[pl.BlockSpec((tm, k), lambda j, i: (i, 0)),
                  pl.BlockSpec((None, k, tn), lambda j, i: (layer, 0, j0 + j))],
        out_specs=pl.BlockSpec((tm, tn), lambda j, i: (i, j)),
        out_shape=jax.ShapeDtypeStruct((m, n), out_dtype),
        scratch_shapes=[pltpu.VMEM((k, tn), BF16)],
        compiler_params=_cparams("parallel", "arbitrary"),
        name="matmul_w32",
    )(a, w32)


def _ln_kernel(x_ref, y_ref, g_ref, b_ref, o32_ref, o16_ref, *, alpha, transposed):
    y = y_ref[...].astype(F32)
    if transposed:
        y = y.T
    h = alpha * x_ref[...] + y
    mu = jnp.mean(h, axis=-1, keepdims=True)
    d = h - mu
    var = jnp.mean(d * d, axis=-1, keepdims=True)
    out = d * lax.rsqrt(var + LN_EPS) * g_ref[...] + b_ref[...]
    o32_ref[...] = out
    if transposed:
        o16_ref[...] = out.astype(BF16)
    else:
        o16_ref[...] = out.T.astype(BF16)


def _residual_ln(x, y, g, b, alpha, transposed, tm=256):
    t, d = x.shape
    tm = _tile(t, tm)
    row = pl.BlockSpec((tm, d), lambda i: (i, 0))
    col = pl.BlockSpec((d, tm), lambda i: (0, i))
    vec = pl.BlockSpec((1, d), lambda i: (0, 0))
    return pl.pallas_call(
        functools.partial(_ln_kernel, alpha=alpha, transposed=transposed),
        grid=(t // tm,),
        in_specs=[row, col if transposed else row, vec, vec],
        out_specs=[row, row if transposed else col],
        out_shape=[jax.ShapeDtypeStruct((t, d), F32),
                   jax.ShapeDtypeStruct((t, d) if transposed else (d, t), BF16)],
        compiler_params=_cparams("parallel"),
        name="residual_ln",
    )(x, y, g.reshape(1, d).astype(F32), b.reshape(1, d).astype(F32))


def _shift_rows(h, k):
    rolled = pltpu.roll(h, k, 0)
    rows = lax.broadcasted_iota(jnp.int32, h.shape, 0)
    return jnp.where(rows >= k, rolled, 0.0)


def _ssd_conv_kernel(x_ref, w_ref, b_ref, o_ref):
    h = x_ref[...].astype(F32)
    w = w_ref[...]
    kw = w.shape[0]
    acc = h * w[kw - 1:kw, :] + b_ref[...]
    for k in range(kw - 1):
        acc = acc + _shift_rows(h, kw - 1 - k) * w[k:k + 1, :]
    o_ref[...] = (acc * jax.nn.sigmoid(acc)).astype(o_ref.dtype)


def _ssd_conv(zx, conv_w, conv_b, batch, seq, d_inner, tc=128):
    t = zx.shape[0]
    cdim = conv_w.shape[1]
    off = d_inner // tc
    return pl.pallas_call(
        _ssd_conv_kernel,
        grid=(batch, cdim // tc),
        in_specs=[pl.BlockSpec((seq, tc), lambda b, j: (b, off + j)),
                  pl.BlockSpec((conv_w.shape[0], tc), lambda b, j: (0, j)),
                  pl.BlockSpec((1, tc), lambda b, j: (0, j))],
        out_specs=pl.BlockSpec((seq, tc), lambda b, j: (b, j)),
        out_shape=jax.ShapeDtypeStruct((t, cdim), BF16),
        compiler_params=_cparams("parallel", "parallel"),
        name="ssd_conv_silu",
    )(zx, conv_w.astype(F32), conv_b.reshape(1, cdim).astype(F32))


def _ssd_scan_kernel(xs_ref, b_ref, c_ref, z_ref, dtraw_ref, dtb_ref, a_ref, sel_ref,
                     dskip_ref, nw_ref, o_ref, state_ref, dt_ref, acs_ref, dte_ref, acse_ref, *, ts, gw):
    nchunk = ts // CHUNK
    g = pl.program_id(2)

    @pl.when(pl.program_id(1) == 0)
    def _():
        state_ref[g] = jnp.zeros(state_ref.shape[1:], F32)

    @pl.when(g == 0)
    def _():
        xr = dtraw_ref[...] + dtb_ref[...]
        dt = jnp.maximum(xr, 0.0) + jnp.log1p(jnp.exp(-jnp.abs(xr)))
        da = dt * a_ref[...]
        ri = lax.broadcasted_iota(jnp.int32, (ts, ts), 0)
        ci = lax.broadcasted_iota(jnp.int32, (ts, ts), 1)
        tril = ((ri >= ci) & (ri // CHUNK == ci // CHUNK)).astype(BF16)
        h1, h2 = _split2(da)
        both = _dot(tril, jnp.concatenate([h1, h2], axis=1))
        nh = da.shape[1]
        dt_ref[...] = dt
        acs_ref[...] = both[:, :nh] + both[:, nh:]

    sel2 = sel_ref[...]
    dte_ref[...] = _dot2(dt_ref[...], sel2)
    acse_ref[...] = _dot2(acs_ref[...], sel2)

    li = lax.broadcasted_iota(jnp.int32, (CHUNK, gw), 0)
    si = lax.broadcasted_iota(jnp.int32, (CHUNK, gw), 1) % CHUNK
    diag = li == si
    causal = li >= si
    r2 = lax.broadcasted_iota(jnp.int32, (2 * CHUNK, 2 * CHUNK), 0) // CHUNK
    c2 = lax.broadcasted_iota(jnp.int32, (2 * CHUNK, 2 * CHUNK), 1) // CHUNK
    blockdiag = r2 == c2
    dskip = dskip_ref[...]
    nw = nw_ref[...]

    def chunk(c, carry):
        rows = pl.ds(pl.multiple_of(c * CHUNK, CHUNK), CHUNK)
        cc = c_ref[rows, :]
        bc = b_ref[rows, :]
        xs = xs_ref[rows, :].astype(F32)
        dte = dte_ref[rows, :]
        acse = acse_ref[rows, :]
        xdt = xs * dte
        acs_last = acse[CHUNK - 1:CHUNK, :]
        acs_t = jnp.sum(jnp.where(diag, acse, 0.0), axis=0, keepdims=True)
        decay = jnp.exp2(jnp.where(causal, acse - acs_t, -jnp.inf))
        bdup = jnp.concatenate([bc, bc], axis=0)
        cb2 = lax.dot_general(cc, bdup, (((1,), (1,)), ((), ())), preferred_element_type=F32)
        xdt16 = xdt.astype(BF16)
        ys = []
        for q in range(gw // (2 * CHUNK)):
            lanes = slice(q * 2 * CHUNK, (q + 1) * 2 * CHUNK)
            wq = (cb2 * decay[:, lanes]).astype(BF16)
            xq = xdt16[:, lanes]
            bd = jnp.where(blockdiag, jnp.concatenate([xq, xq], axis=0), jnp.zeros((), BF16))
            ys.append(_dot(wq, bd))
        y = jnp.concatenate(ys, axis=1) if len(ys) > 1 else ys[0]
        state = state_ref[g]
        y = y + _dot(cc, state.astype(BF16)) * jnp.exp2(acse)
        xdtw = (xdt * jnp.exp2(acs_last - acse)).astype(BF16)
        state_ref[g] = state * jnp.exp2(acs_last) + lax.dot_general(
            bc, xdtw, (((0,), (0,)), ((), ())), preferred_element_type=F32)
        y = y + dskip * xs
        zc = z_ref[rows, :].astype(F32)
        hh = y * (zc * jax.nn.sigmoid(zc))
        ms = jnp.mean(hh * hh, axis=-1, keepdims=True)
        o_ref[rows, :] = (hh * lax.rsqrt(ms + RMS_EPS) * nw).astype(o_ref.dtype)
        return carry

    lax.fori_loop(0, nchunk, chunk, 0, unroll=2)


def _ssd_scan(zx, xbc, dt_raw, dt_bias, a_log, d_skip, norm_w, batch, seq, d_inner, ts=512):
    t = zx.shape[0]
    nheads = dt_raw.shape[1]
    gw = d_inner // SSD_GROUPS
    hpg = nheads // SSD_GROUPS
    assert gw % (2 * CHUNK) == 0 and gw == hpg * SSD_HEADDIM
    ts = min(ts, seq)
    spb = seq // ts
    nb = SSD_STATE // 128
    head_of_lane = jnp.arange(d_inner, dtype=jnp.int32) // SSD_HEADDIM
    sel = (jnp.arange(nheads, dtype=jnp.int32)[:, None] == head_of_lane[None, :]).astype(BF16)
    sel2 = jnp.concatenate([sel, sel], axis=0)
    dskip = jnp.repeat(d_skip.astype(F32), SSD_HEADDIM).reshape(1, d_inner)
    a = -jnp.exp(a_log.astype(F32)).reshape(1, nheads) * LOG2E
    kern = functools.partial(_ssd_scan_kernel, ts=ts, gw=gw)
    boff = d_inner // SSD_STATE
    return pl.pallas_call(
        kern,
        grid=(batch, spb, SSD_GROUPS),
        in_specs=[
            pl.BlockSpec((ts, gw), lambda b, s, g: (b * spb + s, g)),
            pl.BlockSpec((ts, SSD_STATE), lambda b, s, g: (b * spb + s, boff + g * nb)),
            pl.BlockSpec((ts, SSD_STATE), lambda b, s, g: (b * spb + s, boff + (SSD_GROUPS + g) * nb)),
            pl.BlockSpec((ts, gw), lambda b, s, g: (b * spb + s, g)),
            pl.BlockSpec((ts, nheads), lambda b, s, g: (b * spb + s, 0)),
            pl.BlockSpec((1, nheads), lambda b, s, g: (0, 0)),
            pl.BlockSpec((1, nheads), lambda b, s, g: (0, 0)),
            pl.BlockSpec((2 * nheads, gw), lambda b, s, g: (0, g)),
            pl.BlockSpec((1, gw), lambda b, s, g: (0, g)),
            pl.BlockSpec((1, gw), lambda b, s, g: (0, g)),
        ],
        out_specs=pl.BlockSpec((ts, gw), lambda b, s, g: (b * spb + s, g)),
        out_shape=jax.ShapeDtypeStruct((t, d_inner), BF16),
        scratch_shapes=[pltpu.VMEM((SSD_GROUPS, SSD_STATE, gw), F32),
                        pltpu.VMEM((ts, nheads), F32),
                        pltpu.VMEM((ts, nheads), F32),
                        pltpu.VMEM((ts, gw), F32),
                        pltpu.VMEM((ts, gw), F32)],
        compiler_params=_cparams("parallel", "arbitrary", "arbitrary"),
        name="ssd_scan",
    )(xbc, xbc, xbc, zx, dt_raw, dt_bias.reshape(1, nheads).astype(F32), a, sel2, dskip,
      norm_w.reshape(1, d_inner).astype(F32))


def _ssd_mixer(x16, in_proj, out_proj16, layer, conv_w, conv_b, dt_bias, a_log, d_skip, norm_w, batch, seq):
    d_inner = out_proj16.shape[1]
    cdim = conv_w.shape[1]
    zx = _matmul_w32(x16, in_proj, layer, BF16, 1024, 512, n=d_inner + cdim)
    dt_raw = _matmul_w32(x16, in_proj, layer, F32, 1024, 128, col0=d_inner + cdim)
    xbc = _ssd_conv(zx, conv_w, conv_b, batch, seq, d_inner)
    h = _ssd_scan(zx, xbc, dt_raw, dt_bias, a_log, d_skip, norm_w, batch, seq, d_inner)
    return _matmul(h, out_proj16, BF16, 1024, 512, w_layer=layer)


def _sc_gate_kernel(gb_ref, gc_ref, h_ref, w_ref, o_ref):
    u = gc_ref[...].astype(F32) * h_ref[...].astype(F32)
    w = w_ref[...]
    kw = w.shape[0]
    acc = u * w[kw - 1:kw, :]
    for k in range(kw - 1):
        acc = acc + _shift_rows(u, kw - 1 - k) * w[k:k + 1, :]
    o_ref[...] = (gb_ref[...].astype(F32) * acc).astype(o_ref.dtype)


def _sc_gate(bch, conv_w, batch, seq, d, tc=128):
    t = bch.shape[0]
    nj = d // tc
    return pl.pallas_call(
        _sc_gate_kernel,
        grid=(batch, nj),
        in_specs=[pl.BlockSpec((seq, tc), lambda b, j: (b, j)),
                  pl.BlockSpec((seq, tc), lambda b, j: (b, nj + j)),
                  pl.BlockSpec((seq, tc), lambda b, j: (b, 2 * nj + j)),
                  pl.BlockSpec((conv_w.shape[0], tc), lambda b, j: (0, j))],
        out_specs=pl.BlockSpec((seq, tc), lambda b, j: (b, j)),
        out_shape=jax.ShapeDtypeStruct((t, d), BF16),
        compiler_params=_cparams("parallel", "parallel"),
        name="shortconv_gate",
    )(bch, bch, bch, conv_w.astype(F32))


def _shortconv_mixer(x16, in_proj, out_proj, layer, conv_w, batch, seq):
    d = x16.shape[1]
    bch = _matmul_w32(x16, in_proj, layer, BF16, 1024, 512)
    y = _sc_gate(bch, conv_w, batch, seq, d)
    return _matmul_w32(y, out_proj, layer, BF16, 1024, 512)


def _merge_exchange_network(n):
    t = max(1, (n - 1).bit_length())
    pairs = []
    p = 1 << (t - 1)
    while p > 0:
        q, r, d = 1 << (t - 1), 0, p
        while d > 0:
            pairs += [(i, i + d) for i in range(n - d) if (i & p) == r]
            d, q, r = q - p, q // 2, p
        p //= 2
    return pairs


def _top_values(s, k, rows):
    nb = s.shape[0] // SUBLANES
    blocks = [s[b * SUBLANES:(b + 1) * SUBLANES, :] for b in range(nb)]
    for i, j in _merge_exchange_network(nb):
        blocks[i], blocks[j] = jnp.maximum(blocks[i], blocks[j]), jnp.minimum(blocks[i], blocks[j])
    vals = []
    for p in range(k):
        m = jnp.max(blocks[0], axis=0, keepdims=True)
        vals.append(m)
        popped = blocks[0] == m
        for q in range(min(nb, k - 1 - p)):
            below = blocks[q + 1] if q + 1 < nb else -jnp.inf
            blocks[q] = jnp.where(popped, below, blocks[q])
    if rows > k:
        vals.append(jnp.full((rows - k, s.shape[1]), -jnp.inf, F32))
    return jnp.concatenate(vals, axis=0)


def _peer_topk_kernel(qt_ref, keys_ref, s1c_ref, s2_ref, tauc_ref):
    dh = PEER_HALF_DIM
    k = PEER_TOPK

    def head(h, carry):
        r0 = pl.multiple_of(h * 2 * dh, 2 * dh)
        s1 = _dot(keys_ref[h, 0], qt_ref[pl.ds(r0, dh), :]) * LOG2E
        s2 = _dot(keys_ref[h, 1], qt_ref[pl.ds(r0 + dh, dh), :]) * LOG2E
        sa = _top_values(s1, k + 1, 24)
        sb = _top_values(s2, k + 1, 24)
        pieces = [sa[0:1, :] + sb]
        for r in range(1, 8):
            pieces.append(sa[r:r + 1, :] + sb[0:8, :])
        pieces.append(sa[8:24, :] + sb[0:1, :])
        top = _top_values(jnp.concatenate(pieces, axis=0), k + 1, k + 1)
        mx = top[0:1, :]
        z = jnp.sum(jnp.exp2(top[0:k, :] - mx), axis=0, keepdims=True)
        cst = mx + jnp.log(z) * LOG2E
        tau = 0.5 * (top[k - 1:k, :] + top[k:k + 1, :])
        s1c_ref[h] = s1 - cst
        s2_ref[h] = s2
        tauc_ref[h] = tau - cst
        return carry

    lax.fori_loop(0, PEER_HEADS, head, 0)


def _peer_topk(qt, keys16, tm=512):
    t = qt.shape[1]
    tm = _tile(t, tm)
    sc_shape = jax.ShapeDtypeStruct((PEER_HEADS, PEER_NKEYS, t), F32)
    row_shape = jax.ShapeDtypeStruct((PEER_HEADS, 1, t), F32)
    sc_spec = pl.BlockSpec((PEER_HEADS, PEER_NKEYS, tm), lambda i: (0, 0, i))
    row_spec = pl.BlockSpec((PEER_HEADS, 1, tm), lambda i: (0, 0, i))
    return pl.pallas_call(
        _peer_topk_kernel,
        grid=(t // tm,),
        in_specs=[pl.BlockSpec((qt.shape[0], tm), lambda i: (0, i)),
                  pl.BlockSpec(keys16.shape, lambda i: (0, 0, 0, 0))],
        out_specs=[sc_spec, sc_spec, row_spec],
        out_shape=[sc_shape, sc_shape, row_shape],
        compiler_params=_cparams("parallel"),
        name="peer_topk",
    )(qt, keys16)


def _peer_dense_kernel(xt_ref, u_ref, vt_ref, s1c_ref, s2_ref, tauc_ref, o_ref,
                       h_ref, act_ref, s1b_ref, taub_ref, *, te, ne):
    g = pl.program_id(0)
    e1 = jnp.maximum(g - 1, 0) % ne
    e2 = jnp.maximum(g - 2, 0) % ne
    new = g % 2
    old = 1 - new
    nk = PEER_NKEYS
    ni = te // nk
    d, tm = o_ref.shape
    gp = s1b_ref.shape[0]
    db = d // (ni // gp)
    rb = 2 * SUBLANES

    @pl.when(g == 0)
    def _():
        h_ref[...] = jnp.zeros_like(h_ref)
        act_ref[...] = jnp.zeros_like(act_ref)

    @pl.when(e2 == 0)
    def _():
        o_ref[...] = jnp.zeros_like(o_ref)

    @pl.when(e1 == 0)
    def _():
        for h in range(PEER_HEADS):
            taub_ref[h] = jnp.broadcast_to(tauc_ref[h], (SUBLANES, tm))

    h_ref[new] = _dot(u_ref[...], xt_ref[...])

    def piece(k, carry):
        rows = pl.ds(pl.multiple_of(k * db, db), db)
        o_ref[rows, :] += _dot(vt_ref[rows, :], act_ref[old])
        for q in range(gp):
            kq = k * gp + q
            for h in range(PEER_HEADS):
                s1b_ref[q, h] = jnp.broadcast_to(s1c_ref[h, pl.ds(e1 * ni + kq, 1), :], (SUBLANES, tm))
            for jb in range(nk // rb):
                halves = []
                for j0 in (jb * rb, jb * rb + SUBLANES):
                    gate = None
                    for h in range(PEER_HEADS):
                        a = s1b_ref[q, h] + s2_ref[h, j0:j0 + SUBLANES, :]
                        gh = jnp.where(a >= taub_ref[h], jnp.exp2(a), 0.0)
                        gate = gh if gate is None else gate + gh
                    halves.append(gate)
                r0 = pl.multiple_of(kq * nk, nk) + jb * rb
                pre = h_ref[old, pl.ds(r0, rb), :]
                act = 0.5 * pre * (1.0 + lax.erf(pre * (1.0 / math.sqrt(2.0))))
                act_ref[new, pl.ds(r0, rb), :] = (act * jnp.concatenate(halves, axis=0)).astype(BF16)
        return carry

    lax.fori_loop(0, ni // gp, piece, 0)


def _peer_dense(xt, u16, vt16, layer, s1c, s2, tauc, tm=512, te=512, groups_per_piece=1):
    d, t = xt.shape
    nexp = u16.shape[1]
    tm, te = _tile(t, tm), _tile(nexp, te)
    assert te % (PEER_NKEYS * groups_per_piece) == 0 and d % (te // PEER_NKEYS) == 0
    ne = nexp // te
    npairs = (t // tm) * ne
    pair = lambda g, lag: jnp.clip(g - lag, 0, npairs - 1)
    sc_spec = pl.BlockSpec((PEER_HEADS, PEER_NKEYS, tm), lambda g: (0, 0, pair(g, 1) // ne))
    row_spec = pl.BlockSpec((PEER_HEADS, 1, tm), lambda g: (0, 0, pair(g, 1) // ne))
    return pl.pallas_call(
        functools.partial(_peer_dense_kernel, te=te, ne=ne),
        grid=(npairs + 2,),
        in_specs=[pl.BlockSpec((d, tm), lambda g: (0, pair(g, 0) // ne)),
                  pl.BlockSpec((None, te, d), lambda g: (layer, pair(g, 0) % ne, 0)),
                  pl.BlockSpec((None, d, te), lambda g: (layer, 0, pair(g, 2) % ne)),
                  sc_spec, sc_spec, row_spec],
        out_specs=pl.BlockSpec((d, tm), lambda g: (0, pair(g, 2) // ne), pipeline_mode=pl.Buffered(1)),
        out_shape=jax.ShapeDtypeStruct((d, t), F32),
        scratch_shapes=[pltpu.VMEM((2, te, tm), F32),
                        pltpu.VMEM((2, te, tm), BF16),
                        pltpu.VMEM((groups_per_piece, PEER_HEADS, SUBLANES, tm), F32),
                        pltpu.VMEM((PEER_HEADS, SUBLANES, tm), F32)],
        compiler_params=_cparams("arbitrary"),
        name="peer_dense",
    )(xt, u16, vt16, s1c, s2, tauc)


def _peer_ffn(xt, wqt16, subkeys, u16, vt16, layer):
    qt = _matmul(wqt16, xt, BF16, 1024, 1024, a_layer=layer)
    s1c, s2, tauc = _peer_topk(qt, subkeys.astype(BF16))
    return _peer_dense(xt, u16, vt16, layer, s1c, s2, tauc)


def kernel(x, ssd_in_proj, ssd_conv_w, ssd_conv_b, ssd_dt_bias, ssd_A_log, ssd_D, ssd_norm_w,
           ssd_out_proj, sc_in_proj, sc_conv_w, sc_out_proj, peer_wq, peer_subkeys, peer_u,
           peer_v, ln1_g, ln1_b, ln2_g, ln2_b):
    batch, seq, d = x.shape
    depth = peer_wq.shape[0]
    alpha = (2 * depth) ** 0.25
    x32 = x.reshape(batch * seq, d).astype(F32)
    x16 = x32.astype(BF16)
    ssd_out16 = ssd_out_proj.astype(BF16)
    wqt16 = jnp.swapaxes(peer_wq, 1, 2).astype(BF16)
    u16 = peer_u.astype(BF16)
    vt16 = jnp.swapaxes(peer_v, 1, 2).astype(BF16)
    for i in range(depth):
        j = i // 2
        if i % 2 == 0:
            mix = _ssd_mixer(x16, ssd_in_proj, ssd_out16, j, ssd_conv_w[j], ssd_conv_b[j], ssd_dt_bias[j],
                             ssd_A_log[j], ssd_D[j], ssd_norm_w[j], batch, seq)
        else:
            mix = _shortconv_mixer(x16, sc_in_proj, sc_out_proj, j, sc_conv_w[j], batch, seq)
        x32, xt16 = _residual_ln(x32, mix, ln1_g[i], ln1_b[i], alpha, transposed=False)
        ffn_t = _peer_ffn(xt16, wqt16, peer_subkeys[i], u16, vt16, i)
        x32, x16 = _residual_ln(x32, ffn_t, ln2_g[i], ln2_b[i], alpha, transposed=True)
    return x32.reshape(batch, seq, d).astype(x.dtype)
```

```python
import functools
import math

import jax
import jax.numpy as jnp
from jax import lax
from jax.experimental import pallas as pl
from jax.experimental.pallas import tpu as pltpu

F32 = jnp.float32
BF16 = jnp.bfloat16

CHUNK = 64
SSD_HEADDIM = 64
SSD_GROUPS = 8
SSD_STATE = 128
PEER_HEADS = 8
PEER_NKEYS = 128
PEER_TOPK = 16
PEER_HALF_DIM = 128
LN_EPS = 1e-5
RMS_EPS = 1e-5
LOG2E = 1.4426950408889634

SUBLANES = 8
VMEM_LIMIT_BYTES = 56 * 1024 * 1024


def _cparams(*sem):
    return pltpu.CompilerParams(dimension_semantics=sem, vmem_limit_bytes=VMEM_LIMIT_BYTES)


def _tile(n, pref):
    t = min(pref, n)
    while n % t:
        t //= 2
    return t


def _split2(x):
    hi = x.astype(BF16)
    lo = (x - hi.astype(F32)).astype(BF16)
    return hi, lo


def _dot(a, b):
    return jnp.dot(a, b, preferred_element_type=F32)


def _dot2(x, sel2):
    hi, lo = _split2(x)
    return _dot(jnp.concatenate([hi, lo], axis=1), sel2)


def _mm_kernel(a_ref, w_ref, o_ref):
    o_ref[...] = _dot(a_ref[...], w_ref[...]).astype(o_ref.dtype)


def _matmul(a, w, out_dtype, tm, tn, a_layer=None, w_layer=None):
    m, k = a.shape[-2:]
    n = w.shape[-1]
    tm, tn = _tile(m, tm), _tile(n, tn)
    a_spec = (pl.BlockSpec((tm, k), lambda i, j: (i, 0)) if a_layer is None else
              pl.BlockSpec((None, tm, k), lambda i, j: (a_layer, i, 0)))
    w_spec = (pl.BlockSpec((k, tn), lambda i, j: (0, j)) if w_layer is None else
              pl.BlockSpec((None, k, tn), lambda i, j: (w_layer, 0, j)))
    return pl.pallas_call(
        _mm_kernel,
        grid=(m // tm, n // tn),
        in_specs=[a_spec, w_spec],
        out_specs=pl.BlockSpec((tm, tn), lambda i, j: (i, j)),
        out_shape=jax.ShapeDtypeStruct((m, n), out_dtype),
        compiler_params=_cparams("parallel", "arbitrary"),
        name="matmul",
    )(a, w)


def _mm_wcast_kernel(a_ref, w_ref, o_ref, w16_ref):
    @pl.when(pl.program_id(1) == 0)
    def _():
        w16_ref[...] = w_ref[...].astype(BF16)

    o_ref[...] = _dot(a_ref[...], w16_ref[...]).astype(o_ref.dtype)


def _matmul_w32(a, w32, layer, out_dtype, tm, tn, col0=0, n=None):
    m, k = a.shape
    n = w32.shape[2] - col0 if n is None else n
    tm, tn = _tile(m, tm), _tile(n, tn)
    assert col0 % tn == 0
    j0 = col0 // tn
    return pl.pallas_call(
        _mm_wcast_kernel,
        grid=(n // tn, m // tm),
        in_specs=[pl.BlockSpec((tm, k), lambda j, i: (i, 0)),
                  pl.BlockSpec((None, k, tn), lambda j, i: (layer, 0, j0 + j))],
        out_specs=pl.BlockSpec((tm, tn), lambda j, i: (i, j)),
        out_shape=jax.ShapeDtypeStruct((m, n), out_dtype),
        scratch_shapes=[pltpu.VMEM((k, tn), BF16)],
        compiler_params=_cparams("parallel", "arbitrary"),
        name="matmul_w32",
    )(a, w32)


def _ln_kernel(x_ref, y_ref, g_ref, b_ref, o32_ref, o16_ref, *, alpha, transposed):
    y = y_ref[...].astype(F32)
    if transposed:
        y = y.T
    h = alpha * x_ref[...] + y
    mu = jnp.mean(h, axis=-1, keepdims=True)
    d = h - mu
    var = jnp.mean(d * d, axis=-1, keepdims=True)
    out = d * lax.rsqrt(var + LN_EPS) * g_ref[...] + b_ref[...]
    o32_ref[...] = out
    if transposed:
        o16_ref[...] = out.astype(BF16)
    else:
        o16_ref[...] = out.T.astype(BF16)


def _residual_ln(x, y, g, b, alpha, transposed, tm=256):
    t, d = x.shape
    tm = _tile(t, tm)
    row = pl.BlockSpec((tm, d), lambda i: (i, 0))
    col = pl.BlockSpec((d, tm), lambda i: (0, i))
    vec = pl.BlockSpec((1, d), lambda i: (0, 0))
    return pl.pallas_call(
        functools.partial(_ln_kernel, alpha=alpha, transposed=transposed),
        grid=(t // tm,),
        in_specs=[row, col if transposed else row, vec, vec],
        out_specs=[row, row if transposed else col],
        out_shape=[jax.ShapeDtypeStruct((t, d), F32),
                   jax.ShapeDtypeStruct((t, d) if transposed else (d, t), BF16)],
        compiler_params=_cparams("parallel"),
        name="residual_ln",
    )(x, y, g.reshape(1, d).astype(F32), b.reshape(1, d).astype(F32))


def _shift_rows(h, k):
    rolled = pltpu.roll(h, k, 0)
    rows = lax.broadcasted_iota(jnp.int32, h.shape, 0)
    return jnp.where(rows >= k, rolled, 0.0)


def _ssd_conv_kernel(x_ref, w_ref, b_ref, o_ref):
    h = x_ref[...].astype(F32)
    w = w_ref[...]
    kw = w.shape[0]
    acc = h * w[kw - 1:kw, :] + b_ref[...]
    for k in range(kw - 1):
        acc = acc + _shift_rows(h, kw - 1 - k) * w[k:k + 1, :]
    o_ref[...] = (acc * jax.nn.sigmoid(acc)).astype(o_ref.dtype)


def _ssd_conv(zx, conv_w, conv_b, batch, seq, d_inner, tc=256):
    t = zx.shape[0]
    cdim = conv_w.shape[1]
    off = d_inner // tc
    return pl.pallas_call(
        _ssd_conv_kernel,
        grid=(batch, cdim // tc),
        in_specs=[pl.BlockSpec((seq, tc), lambda b, j: (b, off + j)),
                  pl.BlockSpec((conv_w.shape[0], tc), lambda b, j: (0, j)),
                  pl.BlockSpec((1, tc), lambda b, j: (0, j))],
        out_specs=pl.BlockSpec((seq, tc), lambda b, j: (b, j)),
        out_shape=jax.ShapeDtypeStruct((t, cdim), BF16),
        compiler_params=_cparams("parallel", "parallel"),
        name="ssd_conv_silu",
    )(zx, conv_w.astype(F32), conv_b.reshape(1, cdim).astype(F32))


def _ssd_scan_kernel(xs_ref, b_ref, c_ref, z_ref, dtraw_ref, dtb_ref, a_ref, sel_ref,
                     dskip_ref, nw_ref, o_ref, state_ref, dt_ref, acs_ref, dte_ref, acse_ref, *, ts, gw):
    nchunk = ts // CHUNK
    g = pl.program_id(2)

    @pl.when(pl.program_id(1) == 0)
    def _():
        state_ref[g] = jnp.zeros(state_ref.shape[1:], F32)

    @pl.when(g == 0)
    def _():
        xr = dtraw_ref[...] + dtb_ref[...]
        dt = jnp.maximum(xr, 0.0) + jnp.log1p(jnp.exp(-jnp.abs(xr)))
        da = dt * a_ref[...]
        ri = lax.broadcasted_iota(jnp.int32, (ts, ts), 0)
        ci = lax.broadcasted_iota(jnp.int32, (ts, ts), 1)
        tril = ((ri >= ci) & (ri // CHUNK == ci // CHUNK)).astype(BF16)
        h1, h2 = _split2(da)
        both = _dot(tril, jnp.concatenate([h1, h2], axis=1))
        nh = da.shape[1]
        dt_ref[...] = dt
        acs_ref[...] = both[:, :nh] + both[:, nh:]

    sel2 = sel_ref[...]
    dte_ref[...] = _dot2(dt_ref[...], sel2)
    acse_ref[...] = _dot2(acs_ref[...], sel2)

    li = lax.broadcasted_iota(jnp.int32, (CHUNK, gw), 0)
    si = lax.broadcasted_iota(jnp.int32, (CHUNK, gw), 1) % CHUNK
    diag = li == si
    causal = li >= si
    r2 = lax.broadcasted_iota(jnp.int32, (2 * CHUNK, 2 * CHUNK), 0) // CHUNK
    c2 = lax.broadcasted_iota(jnp.int32, (2 * CHUNK, 2 * CHUNK), 1) // CHUNK
    blockdiag = r2 == c2
    dskip = dskip_ref[...]
    nw = nw_ref[...]

    def chunk(c, carry):
        rows = pl.ds(pl.multiple_of(c * CHUNK, CHUNK), CHUNK)
        cc = c_ref[rows, :]
        bc = b_ref[rows, :]
        xs = xs_ref[rows, :].astype(F32)
        dte = dte_ref[rows, :]
        acse = acse_ref[rows, :]
        xdt = xs * dte
        acs_last = acse[CHUNK - 1:CHUNK, :]
        acs_t = jnp.sum(jnp.where(diag, acse, 0.0), axis=0, keepdims=True)
        decay = jnp.exp2(jnp.where(causal, acse - acs_t, -jnp.inf))
        bdup = jnp.concatenate([bc, bc], axis=0)
        cb2 = lax.dot_general(cc, bdup, (((1,), (1,)), ((), ())), preferred_element_type=F32)
        xdt16 = xdt.astype(BF16)
        ys = []
        for q in range(gw // (2 * CHUNK)):
            lanes = slice(q * 2 * CHUNK, (q + 1) * 2 * CHUNK)
            wq = (cb2 * decay[:, lanes]).astype(BF16)
            xq = xdt16[:, lanes]
            bd = jnp.where(blockdiag, jnp.concatenate([xq, xq], axis=0), jnp.zeros((), BF16))
            ys.append(_dot(wq, bd))
        y = jnp.concatenate(ys, axis=1) if len(ys) > 1 else ys[0]
        state = state_ref[g]
        y = y + _dot(cc, state.astype(BF16)) * jnp.exp2(acse)
        xdtw = (xdt * jnp.exp2(acs_last - acse)).astype(BF16)
        state_ref[g] = state * jnp.exp2(acs_last) + lax.dot_general(
            bc, xdtw, (((0,), (0,)), ((), ())), preferred_element_type=F32)
        y = y + dskip * xs
        zc = z_ref[rows, :].astype(F32)
        hh = y * (zc * jax.nn.sigmoid(zc))
        ms = jnp.mean(hh * hh, axis=-1, keepdims=True)
        o_ref[rows, :] = (hh * lax.rsqrt(ms + RMS_EPS) * nw).astype(o_ref.dtype)
        return carry

    lax.fori_loop(0, nchunk, chunk, 0, unroll=2)


def _ssd_scan(zx, xbc, dt_raw, dt_bias, a_log, d_skip, norm_w, batch, seq, d_inner, ts=512):
    t = zx.shape[0]
    nheads = dt_raw.shape[1]
    gw = d_inner // SSD_GROUPS
    hpg = nheads // SSD_GROUPS
    assert gw % (2 * CHUNK) == 0 and gw == hpg * SSD_HEADDIM
    ts = min(ts, seq)
    spb = seq // ts
    nb = SSD_STATE // 128
    head_of_lane = jnp.arange(d_inner, dtype=jnp.int32) // SSD_HEADDIM
    sel = (jnp.arange(nheads, dtype=jnp.int32)[:, None] == head_of_lane[None, :]).astype(BF16)
    sel2 = jnp.concatenate([sel, sel], axis=0)
    dskip = jnp.repeat(d_skip.astype(F32), SSD_HEADDIM).reshape(1, d_inner)
    a = -jnp.exp(a_log.astype(F32)).reshape(1, nheads) * LOG2E
    kern = functools.partial(_ssd_scan_kernel, ts=ts, gw=gw)
    boff = d_inner // SSD_STATE
    return pl.pallas_call(
        kern,
        grid=(batch, spb, SSD_GROUPS),
        in_specs=[
            pl.BlockSpec((ts, gw), lambda b, s, g: (b * spb + s, g)),
            pl.BlockSpec((ts, SSD_STATE), lambda b, s, g: (b * spb + s, boff + g * nb)),
            pl.BlockSpec((ts, SSD_STATE), lambda b, s, g: (b * spb + s, boff + (SSD_GROUPS + g) * nb)),
            pl.BlockSpec((ts, gw), lambda b, s, g: (b * spb + s, g)),
            pl.BlockSpec((ts, nheads), lambda b, s, g: (b * spb + s, 0)),
            pl.BlockSpec((1, nheads), lambda b, s, g: (0, 0)),
            pl.BlockSpec((1, nheads), lambda b, s, g: (0, 0)),
            pl.BlockSpec((2 * nheads, gw), lambda b, s, g: (0, g)),
            pl.BlockSpec((1, gw), lambda b, s, g: (0, g)),
            pl.BlockSpec((1, gw), lambda b, s, g: (0, g)),
        ],
        out_specs=pl.BlockSpec((ts, gw), lambda b, s, g: (b * spb + s, g)),
        out_shape=jax.ShapeDtypeStruct((t, d_inner), BF16),
        scratch_shapes=[pltpu.VMEM((SSD_GROUPS, SSD_STATE, gw), F32),
                        pltpu.VMEM((ts, nheads), F32),
                        pltpu.VMEM((ts, nheads), F32),
                        pltpu.VMEM((ts, gw), F32),
                        pltpu.VMEM((ts, gw), F32)],
        compiler_params=_cparams("parallel", "arbitrary", "arbitrary"),
        name="ssd_scan",
    )(xbc, xbc, xbc, zx, dt_raw, dt_bias.reshape(1, nheads).astype(F32), a, sel2, dskip,
      norm_w.reshape(1, d_inner).astype(F32))


def _ssd_mixer(x16, in_proj, out_proj16, layer, conv_w, conv_b, dt_bias, a_log, d_skip, norm_w, batch, seq):
    d_inner = out_proj16.shape[1]
    cdim = conv_w.shape[1]
    zx = _matmul_w32(x16, in_proj, layer, BF16, 1024, 512, n=d_inner + cdim)
    dt_raw = _matmul_w32(x16, in_proj, layer, F32, 1024, 128, col0=d_inner + cdim)
    xbc = _ssd_conv(zx, conv_w, conv_b, batch, seq, d_inner)
    h = _ssd_scan(zx, xbc, dt_raw, dt_bias, a_log, d_skip, norm_w, batch, seq, d_inner)
    return _matmul(h, out_proj16, BF16, 1024, 512, w_layer=layer)


def _sc_gate_kernel(gb_ref, gc_ref, h_ref, w_ref, o_ref):
    u = gc_ref[...].astype(F32) * h_ref[...].astype(F32)
    w = w_ref[...]
    kw = w.shape[0]
    acc = u * w[kw - 1:kw, :]
    for k in range(kw - 1):
        acc = acc + _shift_rows(u, kw - 1 - k) * w[k:k + 1, :]
    o_ref[...] = (gb_ref[...].astype(F32) * acc).astype(o_ref.dtype)


def _sc_gate(bch, conv_w, batch, seq, d, tc=256):
    t = bch.shape[0]
    nj = d // tc
    return pl.pallas_call(
        _sc_gate_kernel,
        grid=(batch, nj),
        in_specs=[pl.BlockSpec((seq, tc), lambda b, j: (b, j)),
                  pl.BlockSpec((seq, tc), lambda b, j: (b, nj + j)),
                  pl.BlockSpec((seq, tc), lambda b, j: (b, 2 * nj + j)),
                  pl.BlockSpec((conv_w.shape[0], tc), lambda b, j: (0, j))],
        out_specs=pl.BlockSpec((seq, tc), lambda b, j: (b, j)),
        out_shape=jax.ShapeDtypeStruct((t, d), BF16),
        compiler_params=_cparams("parallel", "parallel"),
        name="shortconv_gate",
    )(bch, bch, bch, conv_w.astype(F32))


def _shortconv_mixer(x16, in_proj, out_proj, layer, conv_w, batch, seq):
    d = x16.shape[1]
    bch = _matmul_w32(x16, in_proj, layer, BF16, 1024, 512)
    y = _sc_gate(bch, conv_w, batch, seq, d)
    return _matmul_w32(y, out_proj, layer, BF16, 1024, 512)


def _merge_exchange_network(n):
    t = max(1, (n - 1).bit_length())
    pairs = []
    p = 1 << (t - 1)
    while p > 0:
        q, r, d = 1 << (t - 1), 0, p
        while d > 0:
            pairs += [(i, i + d) for i in range(n - d) if (i & p) == r]
            d, q, r = q - p, q // 2, p
        p //= 2
    return pairs


def _top_values(s, k, rows):
    nb = s.shape[0] // SUBLANES
    blocks = [s[b * SUBLANES:(b + 1) * SUBLANES, :] for b in range(nb)]
    for i, j in _merge_exchange_network(nb):
        blocks[i], blocks[j] = jnp.maximum(blocks[i], blocks[j]), jnp.minimum(blocks[i], blocks[j])
    vals = []
    for p in range(k):
        m = jnp.max(blocks[0], axis=0, keepdims=True)
        vals.append(m)
        popped = blocks[0] == m
        for q in range(min(nb, k - 1 - p)):
            below = blocks[q + 1] if q + 1 < nb else -jnp.inf
            blocks[q] = jnp.where(popped, below, blocks[q])
    if rows > k:
        vals.append(jnp.full((rows - k, s.shape[1]), -jnp.inf, F32))
    return jnp.concatenate(vals, axis=0)


def _peer_topk_kernel(qt_ref, keys_ref, s1c_ref, s2_ref, tauc_ref):
    dh = PEER_HALF_DIM
    k = PEER_TOPK

    def head(h, carry):
        r0 = pl.multiple_of(h * 2 * dh, 2 * dh)
        s1 = _dot(keys_ref[h, 0], qt_ref[pl.ds(r0, dh), :]) * LOG2E
        s2 = _dot(keys_ref[h, 1], qt_ref[pl.ds(r0 + dh, dh), :]) * LOG2E
        sa = _top_values(s1, k + 1, 24)
        sb = _top_values(s2, k + 1, 24)
        pieces = [sa[0:1, :] + sb]
        for r in range(1, 8):
            pieces.append(sa[r:r + 1, :] + sb[0:8, :])
        pieces.append(sa[8:24, :] + sb[0:1, :])
        top = _top_values(jnp.concatenate(pieces, axis=0), k + 1, k + 1)
        mx = top[0:1, :]
        z = jnp.sum(jnp.exp2(top[0:k, :] - mx), axis=0, keepdims=True)
        cst = mx + jnp.log(z) * LOG2E
        tau = 0.5 * (top[k - 1:k, :] + top[k:k + 1, :])
        s1c_ref[h] = s1 - cst
        s2_ref[h] = s2
        tauc_ref[h] = tau - cst
        return carry

    lax.fori_loop(0, PEER_HEADS, head, 0)


def _peer_topk(qt, keys16, tm=512):
    t = qt.shape[1]
    tm = _tile(t, tm)
    sc_shape = jax.ShapeDtypeStruct((PEER_HEADS, PEER_NKEYS, t), F32)
    row_shape = jax.ShapeDtypeStruct((PEER_HEADS, 1, t), F32)
    sc_spec = pl.BlockSpec((PEER_HEADS, PEER_NKEYS, tm), lambda i: (0, 0, i))
    row_spec = pl.BlockSpec((PEER_HEADS, 1, tm), lambda i: (0, 0, i))
    return pl.pallas_call(
        _peer_topk_kernel,
        grid=(t // tm,),
        in_specs=[pl.BlockSpec((qt.shape[0], tm), lambda i: (0, i)),
                  pl.BlockSpec(keys16.shape, lambda i: (0, 0, 0, 0))],
        out_specs=[sc_spec, sc_spec, row_spec],
        out_shape=[sc_shape, sc_shape, row_shape],
        compiler_params=_cparams("parallel"),
        name="peer_topk",
    )(qt, keys16)


def _peer_dense_kernel(xt_ref, u_ref, vt_ref, s1c_ref, s2_ref, tauc_ref, o_ref,
                       h_ref, act_ref, s1b_ref, taub_ref, *, te, ne):
    g = pl.program_id(0)
    e1 = jnp.maximum(g - 1, 0) % ne
    e2 = jnp.maximum(g - 2, 0) % ne
    new = g % 2
    old = 1 - new
    nk = PEER_NKEYS
    ni = te // nk
    d, tm = o_ref.shape
    gp = s1b_ref.shape[0]
    db = d // (ni // gp)
    rb = 2 * SUBLANES

    @pl.when(g == 0)
    def _():
        h_ref[...] = jnp.zeros_like(h_ref)
        act_ref[...] = jnp.zeros_like(act_ref)

    @pl.when(e2 == 0)
    def _():
        o_ref[...] = jnp.zeros_like(o_ref)

    @pl.when(e1 == 0)
    def _():
        for h in range(PEER_HEADS):
            taub_ref[h] = jnp.broadcast_to(tauc_ref[h], (SUBLANES, tm))

    h_ref[new] = _dot(u_ref[...], xt_ref[...])

    def piece(k, carry):
        rows = pl.ds(pl.multiple_of(k * db, db), db)
        o_ref[rows, :] += _dot(vt_ref[rows, :], act_ref[old])
        for q in range(gp):
            kq = k * gp + q
            for h in range(PEER_HEADS):
                s1b_ref[q, h] = jnp.broadcast_to(s1c_ref[h, pl.ds(e1 * ni + kq, 1), :], (SUBLANES, tm))
            for jb in range(nk // rb):
                halves = []
                for j0 in (jb * rb, jb * rb + SUBLANES):
                    gate = None
                    for h in range(PEER_HEADS):
                        a = s1b_ref[q, h] + s2_ref[h, j0:j0 + SUBLANES, :]
                        gh = jnp.where(a >= taub_ref[h], jnp.exp2(a), 0.0)
                        gate = gh if gate is None else gate + gh
                    halves.append(gate)
                r0 = pl.multiple_of(kq * nk, nk) + jb * rb
                pre = h_ref[old, pl.ds(r0, rb), :]
                act = 0.5 * pre * (1.0 + lax.erf(pre * (1.0 / math.sqrt(2.0))))
                act_ref[new, pl.ds(r0, rb), :] = (act * jnp.concatenate(halves, axis=0)).astype(BF16)
        return carry

    lax.fori_loop(0, ni // gp, piece, 0)


def _peer_dense(xt, u16, vt16, layer, s1c, s2, tauc, tm=512, te=512, groups_per_piece=1):
    d, t = xt.shape
    nexp = u16.shape[1]
    tm, te = _tile(t, tm), _tile(nexp, te)
    assert te % (PEER_NKEYS * groups_per_piece) == 0 and d % (te // PEER_NKEYS) == 0
    ne = nexp // te
    npairs = (t // tm) * ne
    pair = lambda g, lag: jnp.clip(g - lag, 0, npairs - 1)
    sc_spec = pl.BlockSpec((PEER_HEADS, PEER_NKEYS, tm), lambda g: (0, 0, pair(g, 1) // ne))
    row_spec = pl.BlockSpec((PEER_HEADS, 1, tm), lambda g: (0, 0, pair(g, 1) // ne))
    return pl.pallas_call(
        functools.partial(_peer_dense_kernel, te=te, ne=ne),
        grid=(npairs + 2,),
        in_specs=[pl.BlockSpec((d, tm), lambda g: (0, pair(g, 0) // ne)),
                  pl.BlockSpec((None, te, d), lambda g: (layer, pair(g, 0) % ne, 0)),
                  pl.BlockSpec((None, d, te), lambda g: (layer, 0, pair(g, 2) % ne)),
                  sc_spec, sc_spec, row_spec],
        out_specs=pl.BlockSpec((d, tm), lambda g: (0, pair(g, 2) // ne), pipeline_mode=pl.Buffered(1)),
        out_shape=jax.ShapeDtypeStruct((d, t), F32),
        scratch_shapes=[pltpu.VMEM((2, te, tm), F32),
                        pltpu.VMEM((2, te, tm), BF16),
                        pltpu.VMEM((groups_per_piece, PEER_HEADS, SUBLANES, tm), F32),
                        pltpu.VMEM((PEER_HEADS, SUBLANES, tm), F32)],
        compiler_params=_cparams("arbitrary"),
        name="peer_dense",
    )(xt, u16, vt16, s1c, s2, tauc)


def _peer_ffn(xt, wqt16, subkeys, u16, vt16, layer):
    qt = _matmul(wqt16, xt, BF16, 1024, 1024, a_layer=layer)
    s1c, s2, tauc = _peer_topk(qt, subkeys.astype(BF16))
    return _peer_dense(xt, u16, vt16, layer, s1c, s2, tauc)


def kernel(x, ssd_in_proj, ssd_conv_w, ssd_conv_b, ssd_dt_bias, ssd_A_log, ssd_D, ssd_norm_w,
           ssd_out_proj, sc_in_proj, sc_conv_w, sc_out_proj, peer_wq, peer_subkeys, peer_u,
           peer_v, ln1_g, ln1_b, ln2_g, ln2_b):
    batch, seq, d = x.shape
    depth = peer_wq.shape[0]
    alpha = (2 * depth) ** 0.25
    x32 = x.reshape(batch * seq, d).astype(F32)
    x16 = x32.astype(BF16)
    ssd_out16 = ssd_out_proj.astype(BF16)
    wqt16 = jnp.swapaxes(peer_wq, 1, 2).astype(BF16)
    u16 = peer_u.astype(BF16)
    vt16 = jnp.swapaxes(peer_v, 1, 2).astype(BF16)
    for i in range(depth):
        j = i // 2
        if i % 2 == 0:
            mix = _ssd_mixer(x16, ssd_in_proj, ssd_out16, j, ssd_conv_w[j], ssd_conv_b[j], ssd_dt_bias[j],
                             ssd_A_log[j], ssd_D[j], ssd_norm_w[j], batch, seq)
        else:
            mix = _shortconv_mixer(x16, sc_in_proj, sc_out_proj, j, sc_conv_w[j], batch, seq)
        x32, xt16 = _residual_ln(x32, mix, ln1_g[i], ln1_b[i], alpha, transposed=False)
        ffn_t = _peer_ffn(xt16, wqt16, peer_subkeys[i], u16, vt16, i)
        x32, x16 = _residual_ln(x32, ffn_t, ln2_g[i], ln2_b[i], alpha, transposed=True)
    return x32.reshape(batch, seq, d).astype(x.dtype)
```

```python
import functools
import math

import jax
import jax.numpy as jnp
from jax import lax
from jax.experimental import pallas as pl
from jax.experimental.pallas import tpu as pltpu

F32 = jnp.float32
BF16 = jnp.bfloat16

CHUNK = 64
SSD_HEADDIM = 64
SSD_GROUPS = 8
SSD_STATE = 128
PEER_HEADS = 8
PEER_NKEYS = 128
PEER_TOPK = 16
PEER_HALF_DIM = 128
LN_EPS = 1e-5
RMS_EPS = 1e-5
LOG2E = 1.4426950408889634

SUBLANES = 8
VMEM_LIMIT_BYTES = 56 * 1024 * 1024


def _cparams(*sem):
    return pltpu.CompilerParams(dimension_semantics=sem, vmem_limit_bytes=VMEM_LIMIT_BYTES)


def _tile(n, pref):
    t = min(pref, n)
    while n % t:
        t //= 2
    return t


def _split2(x):
    hi = x.astype(BF16)
    lo = (x - hi.astype(F32)).astype(BF16)
    return hi, lo


def _dot(a, b):
    return jnp.dot(a, b, preferred_element_type=F32)


def _dot2(x, sel2):
    hi, lo = _split2(x)
    return _dot(jnp.concatenate([hi, lo], axis=1), sel2)


def _mm_kernel(a_ref, w_ref, o_ref):
    o_ref[...] = _dot(a_ref[...], w_ref[...]).astype(o_ref.dtype)


def _matmul(a, w, out_dtype, tm, tn, a_layer=None, w_layer=None):
    m, k = a.shape[-2:]
    n = w.shape[-1]
    tm, tn = _tile(m, tm), _tile(n, tn)
    a_spec = (pl.BlockSpec((tm, k), lambda i, j: (i, 0)) if a_layer is None else
              pl.BlockSpec((None, tm, k), lambda i, j: (a_layer, i, 0)))
    w_spec = (pl.BlockSpec((k, tn), lambda i, j: (0, j)) if w_layer is None else
              pl.BlockSpec((None, k, tn), lambda i, j: (w_layer, 0, j)))
    return pl.pallas_call(
        _mm_kernel,
        grid=(m // tm, n // tn),
        in_specs=[a_spec, w_spec],
        out_specs=pl.BlockSpec((tm, tn), lambda i, j: (i, j)),
        out_shape=jax.ShapeDtypeStruct((m, n), out_dtype),
        compiler_params=_cparams("parallel", "arbitrary"),
        name="matmul",
    )(a, w)


def _mm_wcast_kernel(a_ref, w_ref, o_ref, w16_ref):
    @pl.when(pl.program_id(1) == 0)
    def _():
        w16_ref[...] = w_ref[...].astype(BF16)

    o_ref[...] = _dot(a_ref[...], w16_ref[...]).astype(o_ref.dtype)


def _matmul_w32(a, w32, layer, out_dtype, tm, tn, col0=0, n=None):
    m, k = a.shape
    n = w32.shape[2] - col0 if n is None else n
    tm, tn = _tile(m, tm), _tile(n, tn)
    assert col0 % tn == 0
    j0 = col0 // tn
    return pl.pallas_call(
        _mm_wcast_kernel,
        grid=(n // tn, m // tm),
        in_specs=[pl.BlockSpec((tm, k), lambda j, i: (i, 0)),
                  pl.BlockSpec((None, k, tn), lambda j, i: (layer, 0, j0 + j))],
        out_specs=pl.BlockSpec((tm, tn), lambda j, i: (i, j)),
        out_shape=jax.ShapeDtypeStruct((m, n), out_dtype),
        scratch_shapes=[pltpu.VMEM((k, tn), BF16)],
        compiler_params=_cparams("parallel", "arbitrary"),
        name="matmul_w32",
    )(a, w32)


def _ln_kernel(x_ref, y_ref, g_ref, b_ref, o32_ref, o16_ref, *, alpha, transposed):
    y = y_ref[...].astype(F32)
    if transposed:
        y = y.T
    h = alpha * x_ref[...] + y
    mu = jnp.mean(h, axis=-1, keepdims=True)
    d = h - mu
    var = jnp.mean(d * d, axis=-1, keepdims=True)
    out = d * lax.rsqrt(var + LN_EPS) * g_ref[...] + b_ref[...]
    o32_ref[...] = out
    if transposed:
        o16_ref[...] = out.astype(BF16)
    else:
        o16_ref[...] = out.T.astype(BF16)


def _residual_ln(x, y, g, b, alpha, transposed, tm=256):
    t, d = x.shape
    tm = _tile(t, tm)
    row = pl.BlockSpec((tm, d), lambda i: (i, 0))
    col = pl.BlockSpec((d, tm), lambda i: (0, i))
    vec = pl.BlockSpec((1, d), lambda i: (0, 0))
    return pl.pallas_call(
        functools.partial(_ln_kernel, alpha=alpha, transposed=transposed),
        grid=(t // tm,),
        in_specs=[row, col if transposed else row, vec, vec],
        out_specs=[row, row if transposed else col],
        out_shape=[jax.ShapeDtypeStruct((t, d), F32),
                   jax.ShapeDtypeStruct((t, d) if transposed else (d, t), BF16)],
        compiler_params=_cparams("parallel"),
        name="residual_ln",
    )(x, y, g.reshape(1, d).astype(F32), b.reshape(1, d).astype(F32))


def _shift_rows(h, k):
    rolled = pltpu.roll(h, k, 0)
    rows = lax.broadcasted_iota(jnp.int32, h.shape, 0)
    return jnp.where(rows >= k, rolled, 0.0)


def _ssd_conv_kernel(x_ref, w_ref, b_ref, o_ref):
    h = x_ref[...].astype(F32)
    w = w_ref[...]
    kw = w.shape[0]
    acc = h * w[kw - 1:kw, :] + b_ref[...]
    for k in range(kw - 1):
        acc = acc + _shift_rows(h, kw - 1 - k) * w[k:k + 1, :]
    o_ref[...] = (acc * jax.nn.sigmoid(acc)).astype(o_ref.dtype)


def _ssd_conv(zx, conv_w, conv_b, batch, seq, d_inner, tc=256):
    t = zx.shape[0]
    cdim = conv_w.shape[1]
    off = d_inner // tc
    return pl.pallas_call(
        _ssd_conv_kernel,
        grid=(batch, cdim // tc),
        in_specs=[pl.BlockSpec((seq, tc), lambda b, j: (b, off + j)),
                  pl.BlockSpec((conv_w.shape[0], tc), lambda b, j: (0, j)),
                  pl.BlockSpec((1, tc), lambda b, j: (0, j))],
        out_specs=pl.BlockSpec((seq, tc), lambda b, j: (b, j)),
        out_shape=jax.ShapeDtypeStruct((t, cdim), BF16),
        compiler_params=_cparams("parallel", "parallel"),
        name="ssd_conv_silu",
    )(zx, conv_w.astype(F32), conv_b.reshape(1, cdim).astype(F32))


def _ssd_scan_kernel(xs_ref, b_ref, c_ref, z_ref, dtraw_ref, dtb_ref, a_ref, sel_ref,
                     dskip_ref, nw_ref, o_ref, state_ref, dt_ref, acs_ref, dte_ref, acse_ref, *, ts, gw):
    nchunk = ts // CHUNK
    g = pl.program_id(2)

    @pl.when(pl.program_id(1) == 0)
    def _():
        state_ref[g] = jnp.zeros(state_ref.shape[1:], F32)

    @pl.when(g == 0)
    def _():
        xr = dtraw_ref[...] + dtb_ref[...]
        dt = jnp.maximum(xr, 0.0) + jnp.log1p(jnp.exp(-jnp.abs(xr)))
        da = dt * a_ref[...]
        ri = lax.broadcasted_iota(jnp.int32, (ts, ts), 0)
        ci = lax.broadcasted_iota(jnp.int32, (ts, ts), 1)
        tril = ((ri >= ci) & (ri // CHUNK == ci // CHUNK)).astype(BF16)
        h1, h2 = _split2(da)
        both = _dot(tril, jnp.concatenate([h1, h2], axis=1))
        nh = da.shape[1]
        dt_ref[...] = dt
        acs_ref[...] = both[:, :nh] + both[:, nh:]

    sel2 = sel_ref[...]
    dte_ref[...] = _dot2(dt_ref[...], sel2)
    acse_ref[...] = _dot2(acs_ref[...], sel2)

    li = lax.broadcasted_iota(jnp.int32, (CHUNK, gw), 0)
    si = lax.broadcasted_iota(jnp.int32, (CHUNK, gw), 1) % CHUNK
    diag = li == si
    causal = li >= si
    r2 = lax.broadcasted_iota(jnp.int32, (2 * CHUNK, 2 * CHUNK), 0) // CHUNK
    c2 = lax.broadcasted_iota(jnp.int32, (2 * CHUNK, 2 * CHUNK), 1) // CHUNK
    blockdiag = r2 == c2
    dskip = dskip_ref[...]
    nw = nw_ref[...]

    def chunk(c, carry):
        rows = pl.ds(pl.multiple_of(c * CHUNK, CHUNK), CHUNK)
        cc = c_ref[rows, :]
        bc = b_ref[rows, :]
        xs = xs_ref[rows, :].astype(F32)
        dte = dte_ref[rows, :]
        acse = acse_ref[rows, :]
        xdt = xs * dte
        acs_last = acse[CHUNK - 1:CHUNK, :]
        acs_t = jnp.sum(jnp.where(diag, acse, 0.0), axis=0, keepdims=True)
        decay = jnp.exp2(jnp.where(causal, acse - acs_t, -jnp.inf))
        bdup = jnp.concatenate([bc, bc], axis=0)
        cb2 = lax.dot_general(cc, bdup, (((1,), (1,)), ((), ())), preferred_element_type=F32)
        xdt16 = xdt.astype(BF16)
        ys = []
        for q in range(gw // (2 * CHUNK)):
            lanes = slice(q * 2 * CHUNK, (q + 1) * 2 * CHUNK)
            wq = (cb2 * decay[:, lanes]).astype(BF16)
            xq = xdt16[:, lanes]
            bd = jnp.where(blockdiag, jnp.concatenate([xq, xq], axis=0), jnp.zeros((), BF16))
            ys.append(_dot(wq, bd))
        y = jnp.concatenate(ys, axis=1) if len(ys) > 1 else ys[0]
        state = state_ref[g]
        y = y + _dot(cc, state.astype(BF16)) * jnp.exp2(acse)
        xdtw = (xdt * jnp.exp2(acs_last - acse)).astype(BF16)
        state_ref[g] = state * jnp.exp2(acs_last) + lax.dot_general(
            bc, xdtw, (((0,), (0,)), ((), ())), preferred_element_type=F32)
        y = y + dskip * xs
        zc = z_ref[rows, :].astype(F32)
        hh = y * (zc * jax.nn.sigmoid(zc))
        ms = jnp.mean(hh * hh, axis=-1, keepdims=True)
        o_ref[rows, :] = (hh * lax.rsqrt(ms + RMS_EPS) * nw).astype(o_ref.dtype)
        return carry

    lax.fori_loop(0, nchunk, chunk, 0, unroll=2)


def _ssd_scan(zx, xbc, dt_raw, dt_bias, a_log, d_skip, norm_w, batch, seq, d_inner, ts=512):
    t = zx.shape[0]
    nheads = dt_raw.shape[1]
    gw = d_inner // SSD_GROUPS
    hpg = nheads // SSD_GROUPS
    assert gw % (2 * CHUNK) == 0 and gw == hpg * SSD_HEADDIM
    ts = min(ts, seq)
    spb = seq // ts
    nb = SSD_STATE // 128
    head_of_lane = jnp.arange(d_inner, dtype=jnp.int32) // SSD_HEADDIM
    sel = (jnp.arange(nheads, dtype=jnp.int32)[:, None] == head_of_lane[None, :]).astype(BF16)
    sel2 = jnp.concatenate([sel, sel], axis=0)
    dskip = jnp.repeat(d_skip.astype(F32), SSD_HEADDIM).reshape(1, d_inner)
    a = -jnp.exp(a_log.astype(F32)).reshape(1, nheads) * LOG2E
    kern = functools.partial(_ssd_scan_kernel, ts=ts, gw=gw)
    boff = d_inner // SSD_STATE
    return pl.pallas_call(
        kern,
        grid=(batch, spb, SSD_GROUPS),
        in_specs=[
            pl.BlockSpec((ts, gw), lambda b, s, g: (b * spb + s, g)),
            pl.BlockSpec((ts, SSD_STATE), lambda b, s, g: (b * spb + s, boff + g * nb)),
            pl.BlockSpec((ts, SSD_STATE), lambda b, s, g: (b * spb + s, boff + (SSD_GROUPS + g) * nb)),
            pl.BlockSpec((ts, gw), lambda b, s, g: (b * spb + s, g)),
            pl.BlockSpec((ts, nheads), lambda b, s, g: (b * spb + s, 0)),
            pl.BlockSpec((1, nheads), lambda b, s, g: (0, 0)),
            pl.BlockSpec((1, nheads), lambda b, s, g: (0, 0)),
            pl.BlockSpec((2 * nheads, gw), lambda b, s, g: (0, g)),
            pl.BlockSpec((1, gw), lambda b, s, g: (0, g)),
            pl.BlockSpec((1, gw), lambda b, s, g: (0, g)),
        ],
        out_specs=pl.BlockSpec((ts, gw), lambda b, s, g: (b * spb + s, g)),
        out_shape=jax.ShapeDtypeStruct((t, d_inner), BF16),
        scratch_shapes=[pltpu.VMEM((SSD_GROUPS, SSD_STATE, gw), F32),
                        pltpu.VMEM((ts, nheads), F32),
                        pltpu.VMEM((ts, nheads), F32),
                        pltpu.VMEM((ts, gw), F32),
                        pltpu.VMEM((ts, gw), F32)],
        compiler_params=_cparams("parallel", "arbitrary", "arbitrary"),
        name="ssd_scan",
    )(xbc, xbc, xbc, zx, dt_raw, dt_bias.reshape(1, nheads).astype(F32), a, sel2, dskip,
      norm_w.reshape(1, d_inner).astype(F32))


def _ssd_mixer(x16, in_proj, out_proj16, layer, conv_w, conv_b, dt_bias, a_log, d_skip, norm_w, batch, seq):
    d_inner = out_proj16.shape[1]
    cdim = conv_w.shape[1]
    zx = _matmul_w32(x16, in_proj, layer, BF16, 1024, 512, n=d_inner + cdim)
    dt_raw = _matmul_w32(x16, in_proj, layer, F32, 1024, 128, col0=d_inner + cdim)
    xbc = _ssd_conv(zx, conv_w, conv_b, batch, seq, d_inner)
    h = _ssd_scan(zx, xbc, dt_raw, dt_bias, a_log, d_skip, norm_w, batch, seq, d_inner)
    return _matmul(h, out_proj16, BF16, 1024, 512, w_layer=layer)


def _sc_gate_kernel(gb_ref, gc_ref, h_ref, w_ref, o_ref):
    u = gc_ref[...].astype(F32) * h_ref[...].astype(F32)
    w = w_ref[...]
    kw = w.shape[0]
    acc = u * w[kw - 1:kw, :]
    for k in range(kw - 1):
        acc = acc + _shift_rows(u, kw - 1 - k) * w[k:k + 1, :]
    o_ref[...] = (gb_ref[...].astype(F32) * acc).astype(o_ref.dtype)


def _sc_gate(bch, conv_w, batch, seq, d, tc=256):
    t = bch.shape[0]
    nj = d // tc
    return pl.pallas_call(
        _sc_gate_kernel,
        grid=(batch, nj),
        in_specs=[pl.BlockSpec((seq, tc), lambda b, j: (b, j)),
                  pl.BlockSpec((seq, tc), lambda b, j: (b, nj + j)),
                  pl.BlockSpec((seq, tc), lambda b, j: (b, 2 * nj + j)),
                  pl.BlockSpec((conv_w.shape[0], tc), lambda b, j: (0, j))],
        out_specs=pl.BlockSpec((seq, tc), lambda b, j: (b, j)),
        out_shape=jax.ShapeDtypeStruct((t, d), BF16),
        compiler_params=_cparams("parallel", "parallel"),
        name="shortconv_gate",
    )(bch, bch, bch, conv_w.astype(F32))


def _shortconv_mixer(x16, in_proj, out_proj, layer, conv_w, batch, seq):
    d = x16.shape[1]
    bch = _matmul_w32(x16, in_proj, layer, BF16, 1024, 512)
    y = _sc_gate(bch, conv_w, batch, seq, d)
    return _matmul_w32(y, out_proj, layer, BF16, 1024, 512)


def _merge_exchange_network(n):
    t = max(1, (n - 1).bit_length())
    pairs = []
    p = 1 << (t - 1)
    while p > 0:
        q, r, d = 1 << (t - 1), 0, p
        while d > 0:
            pairs += [(i, i + d) for i in range(n - d) if (i & p) == r]
            d, q, r = q - p, q // 2, p
        p //= 2
    return pairs


def _top_values(s, k, rows):
    nb = s.shape[0] // SUBLANES
    blocks = [s[b * SUBLANES:(b + 1) * SUBLANES, :] for b in range(nb)]
    for i, j in _merge_exchange_network(nb):
        blocks[i], blocks[j] = jnp.maximum(blocks[i], blocks[j]), jnp.minimum(blocks[i], blocks[j])
    vals = []
    for p in range(k):
        m = jnp.max(blocks[0], axis=0, keepdims=True)
        vals.append(m)
        popped = blocks[0] == m
        for q in range(min(nb, k - 1 - p)):
            below = blocks[q + 1] if q + 1 < nb else -jnp.inf
            blocks[q] = jnp.where(popped, below, blocks[q])
    if rows > k:
        vals.append(jnp.full((rows - k, s.shape[1]), -jnp.inf, F32))
    return jnp.concatenate(vals, axis=0)


def _peer_topk_kernel(qt_ref, keys_ref, s1c_ref, s2_ref, tauc_ref):
    dh = PEER_HALF_DIM
    k = PEER_TOPK

    def head(h, carry):
        r0 = pl.multiple_of(h * 2 * dh, 2 * dh)
        s1 = _dot(keys_ref[h, 0], qt_ref[pl.ds(r0, dh), :]) * LOG2E
        s2 = _dot(keys_ref[h, 1], qt_ref[pl.ds(r0 + dh, dh), :]) * LOG2E
        sa = _top_values(s1, k + 1, 24)
        sb = _top_values(s2, k + 1, 24)
        pieces = [sa[0:1, :] + sb]
        for r in range(1, 8):
            pieces.append(sa[r:r + 1, :] + sb[0:8, :])
        pieces.append(sa[8:24, :] + sb[0:1, :])
        top = _top_values(jnp.concatenate(pieces, axis=0), k + 1, k + 1)
        mx = top[0:1, :]
        z = jnp.sum(jnp.exp2(top[0:k, :] - mx), axis=0, keepdims=True)
        cst = mx + jnp.log(z) * LOG2E
        tau = 0.5 * (top[k - 1:k, :] + top[k:k + 1, :])
        s1c_ref[h] = s1 - cst
        s2_ref[h] = s2
        tauc_ref[h] = tau - cst
        return carry

    lax.fori_loop(0, PEER_HEADS, head, 0, unroll=2)


def _peer_topk(qt, keys16, tm=512):
    t = qt.shape[1]
    tm = _tile(t, tm)
    sc_shape = jax.ShapeDtypeStruct((PEER_HEADS, PEER_NKEYS, t), F32)
    row_shape = jax.ShapeDtypeStruct((PEER_HEADS, 1, t), F32)
    sc_spec = pl.BlockSpec((PEER_HEADS, PEER_NKEYS, tm), lambda i: (0, 0, i))
    row_spec = pl.BlockSpec((PEER_HEADS, 1, tm), lambda i: (0, 0, i))
    return pl.pallas_call(
        _peer_topk_kernel,
        grid=(t // tm,),
        in_specs=[pl.BlockSpec((qt.shape[0], tm), lambda i: (0, i)),
                  pl.BlockSpec(keys16.shape, lambda i: (0, 0, 0, 0))],
        out_specs=[sc_spec, sc_spec, row_spec],
        out_shape=[sc_shape, sc_shape, row_shape],
        compiler_params=_cparams("parallel"),
        name="peer_topk",
    )(qt, keys16)


def _peer_dense_kernel(xt_ref, u_ref, vt_ref, s1c_ref, s2_ref, tauc_ref, o_ref,
                       h_ref, act_ref, s1b_ref, taub_ref, *, te, ne):
    g = pl.program_id(0)
    e1 = jnp.maximum(g - 1, 0) % ne
    e2 = jnp.maximum(g - 2, 0) % ne
    new = g % 2
    old = 1 - new
    nk = PEER_NKEYS
    ni = te // nk
    d, tm = o_ref.shape
    gp = s1b_ref.shape[0]
    db = d // (ni // gp)
    rb = 2 * SUBLANES

    @pl.when(g == 0)
    def _():
        h_ref[...] = jnp.zeros_like(h_ref)
        act_ref[...] = jnp.zeros_like(act_ref)

    @pl.when(e2 == 0)
    def _():
        o_ref[...] = jnp.zeros_like(o_ref)

    @pl.when(e1 == 0)
    def _():
        for h in range(PEER_HEADS):
            taub_ref[h] = jnp.broadcast_to(tauc_ref[h], (SUBLANES, tm))

    h_ref[new] = _dot(u_ref[...], xt_ref[...])

    def piece(k, carry):
        rows = pl.ds(pl.multiple_of(k * db, db), db)
        o_ref[rows, :] += _dot(vt_ref[rows, :], act_ref[old])
        for q in range(gp):
            kq = k * gp + q
            for h in range(PEER_HEADS):
                s1b_ref[q, h] = jnp.broadcast_to(s1c_ref[h, pl.ds(e1 * ni + kq, 1), :], (SUBLANES, tm))
            for jb in range(nk // rb):
                halves = []
                for j0 in (jb * rb, jb * rb + SUBLANES):
                    gate = None
                    for h in range(PEER_HEADS):
                        a = s1b_ref[q, h] + s2_ref[h, j0:j0 + SUBLANES, :]
                        gh = jnp.where(a >= taub_ref[h], jnp.exp2(a), 0.0)
                        gate = gh if gate is None else gate + gh
                    halves.append(gate)
                r0 = pl.multiple_of(kq * nk, nk) + jb * rb
                pre = h_ref[old, pl.ds(r0, rb), :]
                act = 0.5 * pre * (1.0 + lax.erf(pre * (1.0 / math.sqrt(2.0))))
                act_ref[new, pl.ds(r0, rb), :] = (act * jnp.concatenate(halves, axis=0)).astype(BF16)
        return carry

    lax.fori_loop(0, ni // gp, piece, 0)


def _peer_dense(xt, u16, vt16, layer, s1c, s2, tauc, tm=512, te=512, groups_per_piece=1):
    d, t = xt.shape
    nexp = u16.shape[1]
    tm, te = _tile(t, tm), _tile(nexp, te)
    assert te % (PEER_NKEYS * groups_per_piece) == 0 and d % (te // PEER_NKEYS) == 0
    ne = nexp // te
    npairs = (t // tm) * ne
    pair = lambda g, lag: jnp.clip(g - lag, 0, npairs - 1)
    sc_spec = pl.BlockSpec((PEER_HEADS, PEER_NKEYS, tm), lambda g: (0, 0, pair(g, 1) // ne))
    row_spec = pl.BlockSpec((PEER_HEADS, 1, tm), lambda g: (0, 0, pair(g, 1) // ne))
    return pl.pallas_call(
        functools.partial(_peer_dense_kernel, te=te, ne=ne),
        grid=(npairs + 2,),
        in_specs=[pl.BlockSpec((d, tm), lambda g: (0, pair(g, 0) // ne)),
                  pl.BlockSpec((None, te, d), lambda g: (layer, pair(g, 0) % ne, 0)),
                  pl.BlockSpec((None, d, te), lambda g: (layer, 0, pair(g, 2) % ne)),
                  sc_spec, sc_spec, row_spec],
        out_specs=pl.BlockSpec((d, tm), lambda g: (0, pair(g, 2) // ne), pipeline_mode=pl.Buffered(1)),
        out_shape=jax.ShapeDtypeStruct((d, t), F32),
        scratch_shapes=[pltpu.VMEM((2, te, tm), F32),
                        pltpu.VMEM((2, te, tm), BF16),
                        pltpu.VMEM((groups_per_piece, PEER_HEADS, SUBLANES, tm), F32),
                        pltpu.VMEM((PEER_HEADS, SUBLANES, tm), F32)],
        compiler_params=_cparams("arbitrary"),
        name="peer_dense",
    )(xt, u16, vt16, s1c, s2, tauc)


def _peer_ffn(xt, wqt16, subkeys, u16, vt16, layer):
    qt = _matmul(wqt16, xt, BF16, 1024, 1024, a_layer=layer)
    s1c, s2, tauc = _peer_topk(qt, subkeys.astype(BF16))
    return _peer_dense(xt, u16, vt16, layer, s1c, s2, tauc)


def kernel(x, ssd_in_proj, ssd_conv_w, ssd_conv_b, ssd_dt_bias, ssd_A_log, ssd_D, ssd_norm_w,
           ssd_out_proj, sc_in_proj, sc_conv_w, sc_out_proj, peer_wq, peer_subkeys, peer_u,
           peer_v, ln1_g, ln1_b, ln2_g, ln2_b):
    batch, seq, d = x.shape
    depth = peer_wq.shape[0]
    alpha = (2 * depth) ** 0.25
    x32 = x.reshape(batch * seq, d).astype(F32)
    x16 = x32.astype(BF16)
    ssd_out16 = ssd_out_proj.astype(BF16)
    wqt16 = jnp.swapaxes(peer_wq, 1, 2).astype(BF16)
    u16 = peer_u.astype(BF16)
    vt16 = jnp.swapaxes(peer_v, 1, 2).astype(BF16)
    for i in range(depth):
        j = i // 2
        if i % 2 == 0:
            mix = _ssd_mixer(x16, ssd_in_proj, ssd_out16, j, ssd_conv_w[j], ssd_conv_b[j], ssd_dt_bias[j],
                             ssd_A_log[j], ssd_D[j], ssd_norm_w[j], batch, seq)
        else:
            mix = _shortconv_mixer(x16, sc_in_proj, sc_out_proj, j, sc_conv_w[j], batch, seq)
        x32, xt16 = _residual_ln(x32, mix, ln1_g[i], ln1_b[i], alpha, transposed=False)
        ffn_t = _peer_ffn(xt16, wqt16, peer_subkeys[i], u16, vt16, i)
        x32, x16 = _residual_ln(x32, ffn_t, ln2_g[i], ln2_b[i], alpha, transposed=True)
    return x32.reshape(batch, seq, d).astype(x.dtype)
```

```python
import functools
import math

import jax
import jax.numpy as jnp
from jax import lax
from jax.experimental import pallas as pl
from jax.experimental.pallas import tpu as pltpu

F32 = jnp.float32
BF16 = jnp.bfloat16

CHUNK = 64
SSD_HEADDIM = 64
SSD_GROUPS = 8
SSD_STATE = 128
PEER_HEADS = 8
PEER_NKEYS = 128
PEER_TOPK = 16
PEER_HALF_DIM = 128
LN_EPS = 1e-5
RMS_EPS = 1e-5
LOG2E = 1.4426950408889634

SUBLANES = 8
VMEM_LIMIT_BYTES = 56 * 1024 * 1024


def _cparams(*sem):
    return pltpu.CompilerParams(dimension_semantics=sem, vmem_limit_bytes=VMEM_LIMIT_BYTES)


def _tile(n, pref):
    t = min(pref, n)
    while n % t:
        t //= 2
    return t


def _split2(x):
    hi = x.astype(BF16)
    lo = (x - hi.astype(F32)).astype(BF16)
    return hi, lo


def _dot(a, b):
    return jnp.dot(a, b, preferred_element_type=F32)


def _dot2(x, sel2):
    hi, lo = _split2(x)
    return _dot(jnp.concatenate([hi, lo], axis=1), sel2)


def _mm_kernel(a_ref, w_ref, o_ref):
    o_ref[...] = _dot(a_ref[...], w_ref[...]).astype(o_ref.dtype)


def _matmul(a, w, out_dtype, tm, tn, a_layer=None, w_layer=None):
    m, k = a.shape[-2:]
    n = w.shape[-1]
    tm, tn = _tile(m, tm), _tile(n, tn)
    a_spec = (pl.BlockSpec((tm, k), lambda i, j: (i, 0)) if a_layer is None else
              pl.BlockSpec((None, tm, k), lambda i, j: (a_layer, i, 0)))
    w_spec = (pl.BlockSpec((k, tn), lambda i, j: (0, j)) if w_layer is None else
              pl.BlockSpec((None, k, tn), lambda i, j: (w_layer, 0, j)))
    return pl.pallas_call(
        _mm_kernel,
        grid=(m // tm, n // tn),
        in_specs=[a_spec, w_spec],
        out_specs=pl.BlockSpec((tm, tn), lambda i, j: (i, j)),
        out_shape=jax.ShapeDtypeStruct((m, n), out_dtype),
        compiler_params=_cparams("parallel", "arbitrary"),
        name="matmul",
    )(a, w)


def _mm_wcast_kernel(a_ref, w_ref, o_ref, w16_ref):
    @pl.when(pl.program_id(1) == 0)
    def _():
        w16_ref[...] = w_ref[...].astype(BF16)

    o_ref[...] = _dot(a_ref[...], w16_ref[...]).astype(o_ref.dtype)


def _matmul_w32(a, w32, layer, out_dtype, tm, tn, col0=0, n=None):
    m, k = a.shape
    n = w32.shape[2] - col0 if n is None else n
    tm, tn = _tile(m, tm), _tile(n, tn)
    assert col0 % tn == 0
    j0 = col0 // tn
    return pl.pallas_call(
        _mm_wcast_kernel,
        grid=(n // tn, m // tm),
        in_specs=[pl.BlockSpec((tm, k), lambda j, i: (i, 0)),
                  pl.BlockSpec((None, k, tn), lambda j, i: (layer, 0, j0 + j))],
        out_specs=pl.BlockSpec((tm, tn), lambda j, i: (i, j)),
        out_shape=jax.ShapeDtypeStruct((m, n), out_dtype),
        scratch_shapes=[pltpu.VMEM((k, tn), BF16)],
        compiler_params=_cparams("parallel", "arbitrary"),
        name="matmul_w32",
    )(a, w32)


def _ln_kernel(x_ref, y_ref, g_ref, b_ref, o32_ref, o16_ref, *, alpha, transposed):
    y = y_ref[...].astype(F32)
    if transposed:
        y = y.T
    h = alpha * x_ref[...] + y
    mu = jnp.mean(h, axis=-1, keepdims=True)
    d = h - mu
    var = jnp.mean(d * d, axis=-1, keepdims=True)
    out = d * lax.rsqrt(var + LN_EPS) * g_ref[...] + b_ref[...]
    o32_ref[...] = out
    if transposed:
        o16_ref[...] = out.astype(BF16)
    else:
        o16_ref[...] = out.T.astype(BF16)


def _residual_ln(x, y, g, b, alpha, transposed, tm=256):
    t, d = x.shape
    tm = _tile(t, tm)
    row = pl.BlockSpec((tm, d), lambda i: (i, 0))
    col = pl.BlockSpec((d, tm), lambda i: (0, i))
    vec = pl.BlockSpec((1, d), lambda i: (0, 0))
    return pl.pallas_call(
        functools.partial(_ln_kernel, alpha=alpha, transposed=transposed),
        grid=(t // tm,),
        in_specs=[row, col if transposed else row, vec, vec],
        out_specs=[row, row if transposed else col],
        out_shape=[jax.ShapeDtypeStruct((t, d), F32),
                   jax.ShapeDtypeStruct((t, d) if transposed else (d, t), BF16)],
        compiler_params=_cparams("parallel"),
        name="residual_ln",
    )(x, y, g.reshape(1, d).astype(F32), b.reshape(1, d).astype(F32))


def _shift_rows(h, k):
    rolled = pltpu.roll(h, k, 0)
    rows = lax.broadcasted_iota(jnp.int32, h.shape, 0)
    return jnp.where(rows >= k, rolled, 0.0)


def _ssd_conv_kernel(x_ref, w_ref, b_ref, o_ref):
    h = x_ref[...].astype(F32)
    w = w_ref[...]
    kw = w.shape[0]
    acc = h * w[kw - 1:kw, :] + b_ref[...]
    for k in range(kw - 1):
        acc = acc + _shift_rows(h, kw - 1 - k) * w[k:k + 1, :]
    o_ref[...] = (acc * jax.nn.sigmoid(acc)).astype(o_ref.dtype)


def _ssd_conv(zx, conv_w, conv_b, batch, seq, d_inner, tc=256):
    t = zx.shape[0]
    cdim = conv_w.shape[1]
    off = d_inner // tc
    return pl.pallas_call(
        _ssd_conv_kernel,
        grid=(batch, cdim // tc),
        in_specs=[pl.BlockSpec((seq, tc), lambda b, j: (b, off + j)),
                  pl.BlockSpec((conv_w.shape[0], tc), lambda b, j: (0, j)),
                  pl.BlockSpec((1, tc), lambda b, j: (0, j))],
        out_specs=pl.BlockSpec((seq, tc), lambda b, j: (b, j)),
        out_shape=jax.ShapeDtypeStruct((t, cdim), BF16),
        compiler_params=_cparams("parallel", "parallel"),
        name="ssd_conv_silu",
    )(zx, conv_w.astype(F32), conv_b.reshape(1, cdim).astype(F32))


def _ssd_scan_kernel(xs_ref, b_ref, c_ref, z_ref, dtraw_ref, dtb_ref, a_ref, sel_ref,
                     dskip_ref, nw_ref, o_ref, state_ref, dt_ref, acs_ref, dte_ref, acse_ref, *, ts, gw):
    nchunk = ts // CHUNK
    g = pl.program_id(2)

    @pl.when(pl.program_id(1) == 0)
    def _():
        state_ref[g] = jnp.zeros(state_ref.shape[1:], F32)

    @pl.when(g == 0)
    def _():
        xr = dtraw_ref[...] + dtb_ref[...]
        dt = jnp.maximum(xr, 0.0) + jnp.log1p(jnp.exp(-jnp.abs(xr)))
        da = dt * a_ref[...]
        ri = lax.broadcasted_iota(jnp.int32, (ts, ts), 0)
        ci = lax.broadcasted_iota(jnp.int32, (ts, ts), 1)
        tril = ((ri >= ci) & (ri // CHUNK == ci // CHUNK)).astype(BF16)
        h1, h2 = _split2(da)
        both = _dot(tril, jnp.concatenate([h1, h2], axis=1))
        nh = da.shape[1]
        dt_ref[...] = dt
        acs_ref[...] = both[:, :nh] + both[:, nh:]

    sel2 = sel_ref[...]
    dte_ref[...] = _dot2(dt_ref[...], sel2)
    acse_ref[...] = _dot2(acs_ref[...], sel2)

    li = lax.broadcasted_iota(jnp.int32, (CHUNK, gw), 0)
    si = lax.broadcasted_iota(jnp.int32, (CHUNK, gw), 1) % CHUNK
    diag = li == si
    causal = li >= si
    r2 = lax.broadcasted_iota(jnp.int32, (2 * CHUNK, 2 * CHUNK), 0) // CHUNK
    c2 = lax.broadcasted_iota(jnp.int32, (2 * CHUNK, 2 * CHUNK), 1) // CHUNK
    blockdiag = r2 == c2
    dskip = dskip_ref[...]
    nw = nw_ref[...]

    def chunk(c, carry):
        rows = pl.ds(pl.multiple_of(c * CHUNK, CHUNK), CHUNK)
        cc = c_ref[rows, :]
        bc = b_ref[rows, :]
        xs = xs_ref[rows, :].astype(F32)
        dte = dte_ref[rows, :]
        acse = acse_ref[rows, :]
        xdt = xs * dte
        acs_last = acse[CHUNK - 1:CHUNK, :]
        acs_t = jnp.sum(jnp.where(diag, acse, 0.0), axis=0, keepdims=True)
        decay = jnp.exp2(jnp.where(causal, acse - acs_t, -jnp.inf))
        bdup = jnp.concatenate([bc, bc], axis=0)
        cb2 = lax.dot_general(cc, bdup, (((1,), (1,)), ((), ())), preferred_element_type=F32)
        xdt16 = xdt.astype(BF16)
        ys = []
        for q in range(gw // (2 * CHUNK)):
            lanes = slice(q * 2 * CHUNK, (q + 1) * 2 * CHUNK)
            wq = (cb2 * decay[:, lanes]).astype(BF16)
            xq = xdt16[:, lanes]
            bd = jnp.where(blockdiag, jnp.concatenate([xq, xq], axis=0), jnp.zeros((), BF16))
            ys.append(_dot(wq, bd))
        y = jnp.concatenate(ys, axis=1) if len(ys) > 1 else ys[0]
        state = state_ref[g]
        y = y + _dot(cc, state.astype(BF16)) * jnp.exp2(acse)
        xdtw = (xdt * jnp.exp2(acs_last - acse)).astype(BF16)
        state_ref[g] = state * jnp.exp2(acs_last) + lax.dot_general(
            bc, xdtw, (((0,), (0,)), ((), ())), preferred_element_type=F32)
        y = y + dskip * xs
        zc = z_ref[rows, :].astype(F32)
        hh = y * (zc * jax.nn.sigmoid(zc))
        ms = jnp.mean(hh * hh, axis=-1, keepdims=True)
        o_ref[rows, :] = (hh * lax.rsqrt(ms + RMS_EPS) * nw).astype(o_ref.dtype)
        return carry

    lax.fori_loop(0, nchunk, chunk, 0, unroll=2)


def _ssd_scan(zx, xbc, dt_raw, dt_bias, a_log, d_skip, norm_w, batch, seq, d_inner, ts=512):
    t = zx.shape[0]
    nheads = dt_raw.shape[1]
    gw = d_inner // SSD_GROUPS
    hpg = nheads // SSD_GROUPS
    assert gw % (2 * CHUNK) == 0 and gw == hpg * SSD_HEADDIM
    ts = min(ts, seq)
    spb = seq // ts
    nb = SSD_STATE // 128
    head_of_lane = jnp.arange(d_inner, dtype=jnp.int32) // SSD_HEADDIM
    sel = (jnp.arange(nheads, dtype=jnp.int32)[:, None] == head_of_lane[None, :]).astype(BF16)
    sel2 = jnp.concatenate([sel, sel], axis=0)
    dskip = jnp.repeat(d_skip.astype(F32), SSD_HEADDIM).reshape(1, d_inner)
    a = -jnp.exp(a_log.astype(F32)).reshape(1, nheads) * LOG2E
    kern = functools.partial(_ssd_scan_kernel, ts=ts, gw=gw)
    boff = d_inner // SSD_STATE
    return pl.pallas_call(
        kern,
        grid=(batch, spb, SSD_GROUPS),
        in_specs=[
            pl.BlockSpec((ts, gw), lambda b, s, g: (b * spb + s, g)),
            pl.BlockSpec((ts, SSD_STATE), lambda b, s, g: (b * spb + s, boff + g * nb)),
            pl.BlockSpec((ts, SSD_STATE), lambda b, s, g: (b * spb + s, boff + (SSD_GROUPS + g) * nb)),
            pl.BlockSpec((ts, gw), lambda b, s, g: (b * spb + s, g)),
            pl.BlockSpec((ts, nheads), lambda b, s, g: (b * spb + s, 0)),
            pl.BlockSpec((1, nheads), lambda b, s, g: (0, 0)),
            pl.BlockSpec((1, nheads), lambda b, s, g: (0, 0)),
            pl.BlockSpec((2 * nheads, gw), lambda b, s, g: (0, g)),
            pl.BlockSpec((1, gw), lambda b, s, g: (0, g)),
            pl.BlockSpec((1, gw), lambda b, s, g: (0, g)),
        ],
        out_specs=pl.BlockSpec((ts, gw), lambda b, s, g: (b * spb + s, g)),
        out_shape=jax.ShapeDtypeStruct((t, d_inner), BF16),
        scratch_shapes=[pltpu.VMEM((SSD_GROUPS, SSD_STATE, gw), F32),
                        pltpu.VMEM((ts, nheads), F32),
                        pltpu.VMEM((ts, nheads), F32),
                        pltpu.VMEM((ts, gw), F32),
                        pltpu.VMEM((ts, gw), F32)],
        compiler_params=_cparams("parallel", "arbitrary", "arbitrary"),
        name="ssd_scan",
    )(xbc, xbc, xbc, zx, dt_raw, dt_bias.reshape(1, nheads).astype(F32), a, sel2, dskip,
      norm_w.reshape(1, d_inner).astype(F32))


def _ssd_mixer(x16, in_proj, out_proj16, layer, conv_w, conv_b, dt_bias, a_log, d_skip, norm_w, batch, seq):
    d_inner = out_proj16.shape[1]
    cdim = conv_w.shape[1]
    zx = _matmul_w32(x16, in_proj, layer, BF16, 1024, 512, n=d_inner + cdim)
    dt_raw = _matmul_w32(x16, in_proj, layer, F32, 1024, 128, col0=d_inner + cdim)
    xbc = _ssd_conv(zx, conv_w, conv_b, batch, seq, d_inner)
    h = _ssd_scan(zx, xbc, dt_raw, dt_bias, a_log, d_skip, norm_w, batch, seq, d_inner)
    return _matmul(h, out_proj16, BF16, 1024, 512, w_layer=layer)


def _sc_gate_kernel(gb_ref, gc_ref, h_ref, w_ref, o_ref):
    u = gc_ref[...].astype(F32) * h_ref[...].astype(F32)
    w = w_ref[...]
    kw = w.shape[0]
    acc = u * w[kw - 1:kw, :]
    for k in range(kw - 1):
        acc = acc + _shift_rows(u, kw - 1 - k) * w[k:k + 1, :]
    o_ref[...] = (gb_ref[...].astype(F32) * acc).astype(o_ref.dtype)


def _sc_gate(bch, conv_w, batch, seq, d, tc=256):
    t = bch.shape[0]
    nj = d // tc
    return pl.pallas_call(
        _sc_gate_kernel,
        grid=(batch, nj),
        in_specs=[pl.BlockSpec((seq, tc), lambda b, j: (b, j)),
                  pl.BlockSpec((seq, tc), lambda b, j: (b, nj + j)),
                  pl.BlockSpec((seq, tc), lambda b, j: (b, 2 * nj + j)),
                  pl.BlockSpec((conv_w.shape[0], tc), lambda b, j: (0, j))],
        out_specs=pl.BlockSpec((seq, tc), lambda b, j: (b, j)),
        out_shape=jax.ShapeDtypeStruct((t, d), BF16),
        compiler_params=_cparams("parallel", "parallel"),
        name="shortconv_gate",
    )(bch, bch, bch, conv_w.astype(F32))


def _shortconv_mixer(x16, in_proj, out_proj, layer, conv_w, batch, seq):
    d = x16.shape[1]
    bch = _matmul_w32(x16, in_proj, layer, BF16, 1024, 512)
    y = _sc_gate(bch, conv_w, batch, seq, d)
    return _matmul_w32(y, out_proj, layer, BF16, 1024, 512)


def _merge_exchange_network(n):
    t = max(1, (n - 1).bit_length())
    pairs = []
    p = 1 << (t - 1)
    while p > 0:
        q, r, d = 1 << (t - 1), 0, p
        while d > 0:
            pairs += [(i, i + d) for i in range(n - d) if (i & p) == r]
            d, q, r = q - p, q // 2, p
        p //= 2
    return pairs


def _top_values(s, k, rows):
    nb = s.shape[0] // SUBLANES
    blocks = [s[b * SUBLANES:(b + 1) * SUBLANES, :] for b in range(nb)]
    for i, j in _merge_exchange_network(nb):
        blocks[i], blocks[j] = jnp.maximum(blocks[i], blocks[j]), jnp.minimum(blocks[i], blocks[j])
    vals = []
    for p in range(k):
        m = jnp.max(blocks[0], axis=0, keepdims=True)
        vals.append(m)
        popped = blocks[0] == m
        for q in range(min(nb, k - 1 - p)):
            below = blocks[q + 1] if q + 1 < nb else -jnp.inf
            blocks[q] = jnp.where(popped, below, blocks[q])
    if rows > k:
        vals.append(jnp.full((rows - k, s.shape[1]), -jnp.inf, F32))
    return jnp.concatenate(vals, axis=0)


def _peer_topk_kernel(qt_ref, keys_ref, s1c_ref, s2_ref, tauc_ref):
    dh = PEER_HALF_DIM
    k = PEER_TOPK

    def head(h, carry):
        r0 = pl.multiple_of(h * 2 * dh, 2 * dh)
        s1 = _dot(keys_ref[h, 0], qt_ref[pl.ds(r0, dh), :]) * LOG2E
        s2 = _dot(keys_ref[h, 1], qt_ref[pl.ds(r0 + dh, dh), :]) * LOG2E
        sa = _top_values(s1, k + 1, 24)
        sb = _top_values(s2, k + 1, 24)
        pieces = [sa[0:1, :] + sb]
        for r in range(1, 8):
            pieces.append(sa[r:r + 1, :] + sb[0:8, :])
        pieces.append(sa[8:24, :] + sb[0:1, :])
        top = _top_values(jnp.concatenate(pieces, axis=0), k + 1, k + 1)
        mx = top[0:1, :]
        z = jnp.sum(jnp.exp2(top[0:k, :] - mx), axis=0, keepdims=True)
        cst = mx + jnp.log(z) * LOG2E
        tau = 0.5 * (top[k - 1:k, :] + top[k:k + 1, :])
        s1c_ref[h] = s1 - cst
        s2_ref[h] = s2
        tauc_ref[h] = tau - cst
        return carry

    lax.fori_loop(0, PEER_HEADS, head, 0, unroll=2)


def _peer_topk(qt, keys16, tm=512):
    t = qt.shape[1]
    tm = _tile(t, tm)
    sc_shape = jax.ShapeDtypeStruct((PEER_HEADS, PEER_NKEYS, t), F32)
    row_shape = jax.ShapeDtypeStruct((PEER_HEADS, 1, t), F32)
    sc_spec = pl.BlockSpec((PEER_HEADS, PEER_NKEYS, tm), lambda i: (0, 0, i))
    row_spec = pl.BlockSpec((PEER_HEADS, 1, tm), lambda i: (0, 0, i))
    return pl.pallas_call(
        _peer_topk_kernel,
        grid=(t // tm,),
        in_specs=[pl.BlockSpec((qt.shape[0], tm), lambda i: (0, i)),
                  pl.BlockSpec(keys16.shape, lambda i: (0, 0, 0, 0))],
        out_specs=[sc_spec, sc_spec, row_spec],
        out_shape=[sc_shape, sc_shape, row_shape],
        compiler_params=_cparams("parallel"),
        name="peer_topk",
    )(qt, keys16)


def _peer_dense_kernel(xt_ref, u_ref, vt_ref, s1c_ref, s2_ref, tauc_ref, o_ref,
                       h_ref, act_ref, s1b_ref, taub_ref, acc_ref, *, te, ne):
    g = pl.program_id(0)
    e1 = jnp.maximum(g - 1, 0) % ne
    e2 = jnp.maximum(g - 2, 0) % ne
    new = g % 2
    old = 1 - new
    nk = PEER_NKEYS
    ni = te // nk
    d, tm = o_ref.shape
    gp = s1b_ref.shape[0]
    db = d // (ni // gp)
    rb = 2 * SUBLANES

    @pl.when(g == 0)
    def _():
        h_ref[...] = jnp.zeros_like(h_ref)
        act_ref[...] = jnp.zeros_like(act_ref)

    @pl.when(e2 == 0)
    def _():
        acc_ref[...] = jnp.zeros_like(acc_ref)

    @pl.when(e1 == 0)
    def _():
        for h in range(PEER_HEADS):
            taub_ref[h] = jnp.broadcast_to(tauc_ref[h], (SUBLANES, tm))

    h_ref[new] = _dot(u_ref[...], xt_ref[...])

    def piece(k, carry):
        rows = pl.ds(pl.multiple_of(k * db, db), db)
        acc_ref[rows, :] += _dot(vt_ref[rows, :], act_ref[old])
        for q in range(gp):
            kq = k * gp + q
            for h in range(PEER_HEADS):
                s1b_ref[q, h] = jnp.broadcast_to(s1c_ref[h, pl.ds(e1 * ni + kq, 1), :], (SUBLANES, tm))
            for jb in range(nk // rb):
                halves = []
                for j0 in (jb * rb, jb * rb + SUBLANES):
                    gate = None
                    for h in range(PEER_HEADS):
                        a = s1b_ref[q, h] + s2_ref[h, j0:j0 + SUBLANES, :]
                        gh = jnp.where(a >= taub_ref[h], jnp.exp2(a), 0.0)
                        gate = gh if gate is None else gate + gh
                    halves.append(gate)
                r0 = pl.multiple_of(kq * nk, nk) + jb * rb
                pre = h_ref[old, pl.ds(r0, rb), :]
                act = 0.5 * pre * (1.0 + lax.erf(pre * (1.0 / math.sqrt(2.0))))
                act_ref[new, pl.ds(r0, rb), :] = (act * jnp.concatenate(halves, axis=0)).astype(BF16)
        return carry

    lax.fori_loop(0, ni // gp, piece, 0)

    @pl.when(e2 == ne - 1)
    def _():
        o_ref[...] = acc_ref[...].astype(o_ref.dtype)


def _peer_dense(xt, u16, vt16, layer, s1c, s2, tauc, tm=512, te=512, groups_per_piece=1):
    d, t = xt.shape
    nexp = u16.shape[1]
    tm, te = _tile(t, tm), _tile(nexp, te)
    assert te % (PEER_NKEYS * groups_per_piece) == 0 and d % (te // PEER_NKEYS) == 0
    ne = nexp // te
    npairs = (t // tm) * ne
    pair = lambda g, lag: jnp.clip(g - lag, 0, npairs - 1)
    sc_spec = pl.BlockSpec((PEER_HEADS, PEER_NKEYS, tm), lambda g: (0, 0, pair(g, 1) // ne))
    row_spec = pl.BlockSpec((PEER_HEADS, 1, tm), lambda g: (0, 0, pair(g, 1) // ne))
    return pl.pallas_call(
        functools.partial(_peer_dense_kernel, te=te, ne=ne),
        grid=(npairs + 2,),
        in_specs=[pl.BlockSpec((d, tm), lambda g: (0, pair(g, 0) // ne)),
                  pl.BlockSpec((None, te, d), lambda g: (layer, pair(g, 0) % ne, 0)),
                  pl.BlockSpec((None, d, te), lambda g: (layer, 0, pair(g, 2) % ne)),
                  sc_spec, sc_spec, row_spec],
        out_specs=pl.BlockSpec((d, tm), lambda g: (0, pair(g, 2) // ne), pipeline_mode=pl.Buffered(1)),
        out_shape=jax.ShapeDtypeStruct((d, t), BF16),
        scratch_shapes=[pltpu.VMEM((2, te, tm), F32),
                        pltpu.VMEM((2, te, tm), BF16),
                        pltpu.VMEM((groups_per_piece, PEER_HEADS, SUBLANES, tm), F32),
                        pltpu.VMEM((PEER_HEADS, SUBLANES, tm), F32),
                        pltpu.VMEM((d, tm), F32)],
        compiler_params=_cparams("arbitrary"),
        name="peer_dense",
    )(xt, u16, vt16, s1c, s2, tauc)


def _peer_ffn(xt, wqt16, subkeys, u16, vt16, layer):
    qt = _matmul(wqt16, xt, BF16, 1024, 1024, a_layer=layer)
    s1c, s2, tauc = _peer_topk(qt, subkeys.astype(BF16))
    return _peer_dense(xt, u16, vt16, layer, s1c, s2, tauc)


def kernel(x, ssd_in_proj, ssd_conv_w, ssd_conv_b, ssd_dt_bias, ssd_A_log, ssd_D, ssd_norm_w,
           ssd_out_proj, sc_in_proj, sc_conv_w, sc_out_proj, peer_wq, peer_subkeys, peer_u,
           peer_v, ln1_g, ln1_b, ln2_g, ln2_b):
    batch, seq, d = x.shape
    depth = peer_wq.shape[0]
    alpha = (2 * depth) ** 0.25
    x32 = x.reshape(batch * seq, d).astype(F32)
    x16 = x32.astype(BF16)
    ssd_out16 = ssd_out_proj.astype(BF16)
    wqt16 = jnp.swapaxes(peer_wq, 1, 2).astype(BF16)
    u16 = peer_u.astype(BF16)
    vt16 = jnp.swapaxes(peer_v, 1, 2).astype(BF16)
    for i in range(depth):
        j = i // 2
        if i % 2 == 0:
            mix = _ssd_mixer(x16, ssd_in_proj, ssd_out16, j, ssd_conv_w[j], ssd_conv_b[j], ssd_dt_bias[j],
                             ssd_A_log[j], ssd_D[j], ssd_norm_w[j], batch, seq)
        else:
            mix = _shortconv_mixer(x16, sc_in_proj, sc_out_proj, j, sc_conv_w[j], batch, seq)
        x32, xt16 = _residual_ln(x32, mix, ln1_g[i], ln1_b[i], alpha, transposed=False)
        ffn_t = _peer_ffn(xt16, wqt16, peer_subkeys[i], u16, vt16, i)
        x32, x16 = _residual_ln(x32, ffn_t, ln2_g[i], ln2_b[i], alpha, transposed=True)
    return x32.reshape(batch, seq, d).astype(x.dtype)
```

```python
import functools
import math

import jax
import jax.numpy as jnp
from jax import lax
from jax.experimental import pallas as pl
from jax.experimental.pallas import tpu as pltpu

F32 = jnp.float32
BF16 = jnp.bfloat16

CHUNK = 64
SSD_HEADDIM = 64
SSD_GROUPS = 8
SSD_STATE = 128
PEER_HEADS = 8
PEER_NKEYS = 128
PEER_TOPK = 16
PEER_HALF_DIM = 128
LN_EPS = 1e-5
RMS_EPS = 1e-5
LOG2E = 1.4426950408889634

SUBLANES = 8
VMEM_LIMIT_BYTES = 56 * 1024 * 1024


def _cparams(*sem):
    return pltpu.CompilerParams(dimension_semantics=sem, vmem_limit_bytes=VMEM_LIMIT_BYTES)


def _tile(n, pref):
    t = min(pref, n)
    while n % t:
        t //= 2
    return t


def _split2(x):
    hi = x.astype(BF16)
    lo = (x - hi.astype(F32)).astype(BF16)
    return hi, lo


def _dot(a, b):
    return jnp.dot(a, b, preferred_element_type=F32)


def _dot2(x, sel2):
    hi, lo = _split2(x)
    return _dot(jnp.concatenate([hi, lo], axis=1), sel2)


def _mm_kernel(a_ref, w_ref, o_ref):
    o_ref[...] = _dot(a_ref[...], w_ref[...]).astype(o_ref.dtype)


def _matmul(a, w, out_dtype, tm, tn, a_layer=None, w_layer=None):
    m, k = a.shape[-2:]
    n = w.shape[-1]
    tm, tn = _tile(m, tm), _tile(n, tn)
    a_spec = (pl.BlockSpec((tm, k), lambda i, j: (i, 0)) if a_layer is None else
              pl.BlockSpec((None, tm, k), lambda i, j: (a_layer, i, 0)))
    w_spec = (pl.BlockSpec((k, tn), lambda i, j: (0, j)) if w_layer is None else
              pl.BlockSpec((None, k, tn), lambda i, j: (w_layer, 0, j)))
    return pl.pallas_call(
        _mm_kernel,
        grid=(m // tm, n // tn),
        in_specs=[a_spec, w_spec],
        out_specs=pl.BlockSpec((tm, tn), lambda i, j: (i, j)),
        out_shape=jax.ShapeDtypeStruct((m, n), out_dtype),
        compiler_params=_cparams("parallel", "arbitrary"),
        name="matmul",
    )(a, w)


def _mm_wcast_kernel(a_ref, w_ref, o_ref, w16_ref):
    @pl.when(pl.program_id(1) == 0)
    def _():
        w16_ref[...] = w_ref[...].astype(BF16)

    o_ref[...] = _dot(a_ref[...], w16_ref[...]).astype(o_ref.dtype)


def _matmul_w32(a, w32, layer, out_dtype, tm, tn, col0=0, n=None):
    m, k = a.shape
    n = w32.shape[2] - col0 if n is None else n
    tm, tn = _tile(m, tm), _tile(n, tn)
    assert col0 % tn == 0
    j0 = col0 // tn
    return pl.pallas_call(
        _mm_wcast_kernel,
        grid=(n // tn, m // tm),
        in_specs=[pl.BlockSpec((tm, k), lambda j, i: (i, 0)),
                  pl.BlockSpec((None, k, tn), lambda j, i: (layer, 0, j0 + j))],
        out_specs=pl.BlockSpec((tm, tn), lambda j, i: (i, j)),
        out_shape=jax.ShapeDtypeStruct((m, n), out_dtype),
        scratch_shapes=[pltpu.VMEM((k, tn), BF16)],
        compiler_params=_cparams("parallel", "arbitrary"),
        name="matmul_w32",
    )(a, w32)


def _ln_kernel(x_ref, y_ref, g_ref, b_ref, o32_ref, o16_ref, *, alpha, transposed):
    y = y_ref[...].astype(F32)
    if transposed:
        y = y.T
    h = alpha * x_ref[...] + y
    mu = jnp.mean(h, axis=-1, keepdims=True)
    d = h - mu
    var = jnp.mean(d * d, axis=-1, keepdims=True)
    out = d * lax.rsqrt(var + LN_EPS) * g_ref[...] + b_ref[...]
    o32_ref[...] = out
    if transposed:
        o16_ref[...] = out.astype(BF16)
    else:
        o16_ref[...] = out.T.astype(BF16)


def _residual_ln(x, y, g, b, alpha, transposed, tm=256):
    t, d = x.shape
    tm = _tile(t, tm)
    row = pl.BlockSpec((tm, d), lambda i: (i, 0))
    col = pl.BlockSpec((d, tm), lambda i: (0, i))
    vec = pl.BlockSpec((1, d), lambda i: (0, 0))
    return pl.pallas_call(
        functools.partial(_ln_kernel, alpha=alpha, transposed=transposed),
        grid=(t // tm,),
        in_specs=[row, col if transposed else row, vec, vec],
        out_specs=[row, row if transposed else col],
        out_shape=[jax.ShapeDtypeStruct((t, d), F32),
                   jax.ShapeDtypeStruct((t, d) if transposed else (d, t), BF16)],
        compiler_params=_cparams("parallel"),
        name="residual_ln",
    )(x, y, g.reshape(1, d).astype(F32), b.reshape(1, d).astype(F32))


def _shift_rows(h, k):
    rolled = pltpu.roll(h, k, 0)
    rows = lax.broadcasted_iota(jnp.int32, h.shape, 0)
    return jnp.where(rows >= k, rolled, 0.0)


def _ssd_conv_kernel(x_ref, w_ref, b_ref, o_ref):
    h = x_ref[...].astype(F32)
    w = w_ref[...]
    kw = w.shape[0]
    acc = h * w[kw - 1:kw, :] + b_ref[...]
    for k in range(kw - 1):
        acc = acc + _shift_rows(h, kw - 1 - k) * w[k:k + 1, :]
    o_ref[...] = (acc * jax.nn.sigmoid(acc)).astype(o_ref.dtype)


def _ssd_conv(zx, conv_w, conv_b, batch, seq, d_inner, tc=256):
    t = zx.shape[0]
    cdim = conv_w.shape[1]
    off = d_inner // tc
    return pl.pallas_call(
        _ssd_conv_kernel,
        grid=(batch, cdim // tc),
        in_specs=[pl.BlockSpec((seq, tc), lambda b, j: (b, off + j)),
                  pl.BlockSpec((conv_w.shape[0], tc), lambda b, j: (0, j)),
                  pl.BlockSpec((1, tc), lambda b, j: (0, j))],
        out_specs=pl.BlockSpec((seq, tc), lambda b, j: (b, j)),
        out_shape=jax.ShapeDtypeStruct((t, cdim), BF16),
        compiler_params=_cparams("parallel", "parallel"),
        name="ssd_conv_silu",
    )(zx, conv_w.astype(F32), conv_b.reshape(1, cdim).astype(F32))


def _ssd_scan_kernel(xs_ref, b_ref, c_ref, z_ref, dtraw_ref, dtb_ref, a_ref, sel_ref,
                     dskip_ref, nw_ref, o_ref, state_ref, dt_ref, acs_ref, dte_ref, acse_ref, *, ts, gw):
    nchunk = ts // CHUNK
    g = pl.program_id(2)

    @pl.when(pl.program_id(1) == 0)
    def _():
        state_ref[g] = jnp.zeros(state_ref.shape[1:], F32)

    @pl.when(g == 0)
    def _():
        xr = dtraw_ref[...] + dtb_ref[...]
        dt = jnp.maximum(xr, 0.0) + jnp.log1p(jnp.exp(-jnp.abs(xr)))
        da = dt * a_ref[...]
        ri = lax.broadcasted_iota(jnp.int32, (ts, ts), 0)
        ci = lax.broadcasted_iota(jnp.int32, (ts, ts), 1)
        tril = ((ri >= ci) & (ri // CHUNK == ci // CHUNK)).astype(BF16)
        h1, h2 = _split2(da)
        both = _dot(tril, jnp.concatenate([h1, h2], axis=1))
        nh = da.shape[1]
        dt_ref[...] = dt
        acs_ref[...] = both[:, :nh] + both[:, nh:]

    sel2 = sel_ref[...]
    dte_ref[...] = _dot2(dt_ref[...], sel2)
    acse_ref[...] = _dot2(acs_ref[...], sel2)

    li = lax.broadcasted_iota(jnp.int32, (CHUNK, gw), 0)
    si = lax.broadcasted_iota(jnp.int32, (CHUNK, gw), 1) % CHUNK
    diag = li == si
    causal = li >= si
    r2 = lax.broadcasted_iota(jnp.int32, (2 * CHUNK, 2 * CHUNK), 0) // CHUNK
    c2 = lax.broadcasted_iota(jnp.int32, (2 * CHUNK, 2 * CHUNK), 1) // CHUNK
    blockdiag = r2 == c2
    dskip = dskip_ref[...]
    nw = nw_ref[...]

    def chunk(c, carry):
        rows = pl.ds(pl.multiple_of(c * CHUNK, CHUNK), CHUNK)
        cc = c_ref[rows, :]
        bc = b_ref[rows, :]
        xs = xs_ref[rows, :].astype(F32)
        dte = dte_ref[rows, :]
        acse = acse_ref[rows, :]
        xdt = xs * dte
        acs_last = acse[CHUNK - 1:CHUNK, :]
        acs_t = jnp.sum(jnp.where(diag, acse, 0.0), axis=0, keepdims=True)
        decay = jnp.exp2(jnp.where(causal, acse - acs_t, -jnp.inf))
        bdup = jnp.concatenate([bc, bc], axis=0)
        cb2 = lax.dot_general(cc, bdup, (((1,), (1,)), ((), ())), preferred_element_type=F32)
        xdt16 = xdt.astype(BF16)
        ys = []
        for q in range(gw // (2 * CHUNK)):
            lanes = slice(q * 2 * CHUNK, (q + 1) * 2 * CHUNK)
            wq = (cb2 * decay[:, lanes]).astype(BF16)
            xq = xdt16[:, lanes]
            bd = jnp.where(blockdiag, jnp.concatenate([xq, xq], axis=0), jnp.zeros((), BF16))
            ys.append(_dot(wq, bd))
        y = jnp.concatenate(ys, axis=1) if len(ys) > 1 else ys[0]
        state = state_ref[g]
        y = y + _dot(cc, state.astype(BF16)) * jnp.exp2(acse)
        xdtw = (xdt * jnp.exp2(acs_last - acse)).astype(BF16)
        state_ref[g] = state * jnp.exp2(acs_last) + lax.dot_general(
            bc, xdtw, (((0,), (0,)), ((), ())), preferred_element_type=F32)
        y = y + dskip * xs
        zc = z_ref[rows, :].astype(F32)
        hh = y * (zc * jax.nn.sigmoid(zc))
        ms = jnp.mean(hh * hh, axis=-1, keepdims=True)
        o_ref[rows, :] = (hh * lax.rsqrt(ms + RMS_EPS) * nw).astype(o_ref.dtype)
        return carry

    lax.fori_loop(0, nchunk, chunk, 0, unroll=2)


def _ssd_scan(zx, xbc, dt_raw, dt_bias, a_log, d_skip, norm_w, batch, seq, d_inner, ts=512):
    t = zx.shape[0]
    nheads = dt_raw.shape[1]
    gw = d_inner // SSD_GROUPS
    hpg = nheads // SSD_GROUPS
    assert gw % (2 * CHUNK) == 0 and gw == hpg * SSD_HEADDIM
    ts = min(ts, seq)
    spb = seq // ts
    nb = SSD_STATE // 128
    head_of_lane = jnp.arange(d_inner, dtype=jnp.int32) // SSD_HEADDIM
    sel = (jnp.arange(nheads, dtype=jnp.int32)[:, None] == head_of_lane[None, :]).astype(BF16)
    sel2 = jnp.concatenate([sel, sel], axis=0)
    dskip = jnp.repeat(d_skip.astype(F32), SSD_HEADDIM).reshape(1, d_inner)
    a = -jnp.exp(a_log.astype(F32)).reshape(1, nheads) * LOG2E
    kern = functools.partial(_ssd_scan_kernel, ts=ts, gw=gw)
    boff = d_inner // SSD_STATE
    return pl.pallas_call(
        kern,
        grid=(batch, spb, SSD_GROUPS),
        in_specs=[
            pl.BlockSpec((ts, gw), lambda b, s, g: (b * spb + s, g)),
            pl.BlockSpec((ts, SSD_STATE), lambda b, s, g: (b * spb + s, boff + g * nb)),
            pl.BlockSpec((ts, SSD_STATE), lambda b, s, g: (b * spb + s, boff + (SSD_GROUPS + g) * nb)),
            pl.BlockSpec((ts, gw), lambda b, s, g: (b * spb + s, g)),
            pl.BlockSpec((ts, nheads), lambda b, s, g: (b * spb + s, 0)),
            pl.BlockSpec((1, nheads), lambda b, s, g: (0, 0)),
            pl.BlockSpec((1, nheads), lambda b, s, g: (0, 0)),
            pl.BlockSpec((2 * nheads, gw), lambda b, s, g: (0, g)),
            pl.BlockSpec((1, gw), lambda b, s, g: (0, g)),
            pl.BlockSpec((1, gw), lambda b, s, g: (0, g)),
        ],
        out_specs=pl.BlockSpec((ts, gw), lambda b, s, g: (b * spb + s, g)),
        out_shape=jax.ShapeDtypeStruct((t, d_inner), BF16),
        scratch_shapes=[pltpu.VMEM((SSD_GROUPS, SSD_STATE, gw), F32),
                        pltpu.VMEM((ts, nheads), F32),
                        pltpu.VMEM((ts, nheads), F32),
                        pltpu.VMEM((ts, gw), F32),
                        pltpu.VMEM((ts, gw), F32)],
        compiler_params=_cparams("parallel", "arbitrary", "arbitrary"),
        name="ssd_scan",
    )(xbc, xbc, xbc, zx, dt_raw, dt_bias.reshape(1, nheads).astype(F32), a, sel2, dskip,
      norm_w.reshape(1, d_inner).astype(F32))


def _ssd_mixer(x16, in_proj, out_proj16, layer, conv_w, conv_b, dt_bias, a_log, d_skip, norm_w, batch, seq):
    d_inner = out_proj16.shape[1]
    cdim = conv_w.shape[1]
    zx = _matmul_w32(x16, in_proj, layer, BF16, 1024, 512, n=d_inner + cdim)
    dt_raw = _matmul_w32(x16, in_proj, layer, F32, 1024, 128, col0=d_inner + cdim)
    xbc = _ssd_conv(zx, conv_w, conv_b, batch, seq, d_inner)
    h = _ssd_scan(zx, xbc, dt_raw, dt_bias, a_log, d_skip, norm_w, batch, seq, d_inner)
    return _matmul(h, out_proj16, BF16, 1024, 512, w_layer=layer)


def _sc_gate_kernel(gb_ref, gc_ref, h_ref, w_ref, o_ref):
    u = gc_ref[...].astype(F32) * h_ref[...].astype(F32)
    w = w_ref[...]
    kw = w.shape[0]
    acc = u * w[kw - 1:kw, :]
    for k in range(kw - 1):
        acc = acc + _shift_rows(u, kw - 1 - k) * w[k:k + 1, :]
    o_ref[...] = (gb_ref[...].astype(F32) * acc).astype(o_ref.dtype)


def _sc_gate(bch, conv_w, batch, seq, d, tc=256):
    t = bch.shape[0]
    nj = d // tc
    return pl.pallas_call(
        _sc_gate_kernel,
        grid=(batch, nj),
        in_specs=[pl.BlockSpec((seq, tc), lambda b, j: (b, j)),
                  pl.BlockSpec((seq, tc), lambda b, j: (b, nj + j)),
                  pl.BlockSpec((seq, tc), lambda b, j: (b, 2 * nj + j)),
                  pl.BlockSpec((conv_w.shape[0], tc), lambda b, j: (0, j))],
        out_specs=pl.BlockSpec((seq, tc), lambda b, j: (b, j)),
        out_shape=jax.ShapeDtypeStruct((t, d), BF16),
        compiler_params=_cparams("parallel", "parallel"),
        name="shortconv_gate",
    )(bch, bch, bch, conv_w.astype(F32))


def _shortconv_mixer(x16, in_proj, out_proj, layer, conv_w, batch, seq):
    d = x16.shape[1]
    bch = _matmul_w32(x16, in_proj, layer, BF16, 1024, 512)
    y = _sc_gate(bch, conv_w, batch, seq, d)
    return _matmul_w32(y, out_proj, layer, BF16, 1024, 512)


def _merge_exchange_network(n):
    t = max(1, (n - 1).bit_length())
    pairs = []
    p = 1 << (t - 1)
    while p > 0:
        q, r, d = 1 << (t - 1), 0, p
        while d > 0:
            pairs += [(i, i + d) for i in range(n - d) if (i & p) == r]
            d, q, r = q - p, q // 2, p
        p //= 2
    return pairs


def _top_values(s, k, rows):
    nb = s.shape[0] // SUBLANES
    blocks = [s[b * SUBLANES:(b + 1) * SUBLANES, :] for b in range(nb)]
    for i, j in _merge_exchange_network(nb):
        blocks[i], blocks[j] = jnp.maximum(blocks[i], blocks[j]), jnp.minimum(blocks[i], blocks[j])
    vals = []
    for p in range(k):
        m = jnp.max(blocks[0], axis=0, keepdims=True)
        vals.append(m)
        popped = blocks[0] == m
        for q in range(min(nb, k - 1 - p)):
            below = blocks[q + 1] if q + 1 < nb else -jnp.inf
            blocks[q] = jnp.where(popped, below, blocks[q])
    if rows > k:
        vals.append(jnp.full((rows - k, s.shape[1]), -jnp.inf, F32))
    return jnp.concatenate(vals, axis=0)


def _peer_topk_kernel(qt_ref, keys_ref, s1c_ref, s2_ref, tauc_ref):
    dh = PEER_HALF_DIM
    k = PEER_TOPK

    def head(h, carry):
        r0 = pl.multiple_of(h * 2 * dh, 2 * dh)
        s1 = _dot(keys_ref[h, 0], qt_ref[pl.ds(r0, dh), :]) * LOG2E
        s2 = _dot(keys_ref[h, 1], qt_ref[pl.ds(r0 + dh, dh), :]) * LOG2E
        sa = _top_values(s1, k + 1, 24)
        sb = _top_values(s2, k + 1, 24)
        pieces = [sa[0:1, :] + sb]
        for r in range(1, 8):
            pieces.append(sa[r:r + 1, :] + sb[0:8, :])
        pieces.append(sa[8:24, :] + sb[0:1, :])
        top = _top_values(jnp.concatenate(pieces, axis=0), k + 1, k + 1)
        mx = top[0:1, :]
        z = jnp.sum(jnp.exp2(top[0:k, :] - mx), axis=0, keepdims=True)
        cst = mx + jnp.log(z) * LOG2E
        tau = 0.5 * (top[k - 1:k, :] + top[k:k + 1, :])
        s1c_ref[h] = s1 - cst
        s2_ref[h] = s2
        tauc_ref[h] = tau - cst
        return carry

    lax.fori_loop(0, PEER_HEADS, head, 0, unroll=4)


def _peer_topk(qt, keys16, tm=512):
    t = qt.shape[1]
    tm = _tile(t, tm)
    sc_shape = jax.ShapeDtypeStruct((PEER_HEADS, PEER_NKEYS, t), F32)
    row_shape = jax.ShapeDtypeStruct((PEER_HEADS, 1, t), F32)
    sc_spec = pl.BlockSpec((PEER_HEADS, PEER_NKEYS, tm), lambda i: (0, 0, i))
    row_spec = pl.BlockSpec((PEER_HEADS, 1, tm), lambda i: (0, 0, i))
    return pl.pallas_call(
        _peer_topk_kernel,
        grid=(t // tm,),
        in_specs=[pl.BlockSpec((qt.shape[0], tm), lambda i: (0, i)),
                  pl.BlockSpec(keys16.shape, lambda i: (0, 0, 0, 0))],
        out_specs=[sc_spec, sc_spec, row_spec],
        out_shape=[sc_shape, sc_shape, row_shape],
        compiler_params=_cparams("parallel"),
        name="peer_topk",
    )(qt, keys16)


def _peer_dense_kernel(xt_ref, u_ref, vt_ref, s1c_ref, s2_ref, tauc_ref, o_ref,
                       h_ref, act_ref, s1b_ref, taub_ref, acc_ref, *, te, ne):
    g = pl.program_id(0)
    e1 = jnp.maximum(g - 1, 0) % ne
    e2 = jnp.maximum(g - 2, 0) % ne
    new = g % 2
    old = 1 - new
    nk = PEER_NKEYS
    ni = te // nk
    d, tm = o_ref.shape
    gp = s1b_ref.shape[0]
    db = d // (ni // gp)
    rb = 2 * SUBLANES

    @pl.when(g == 0)
    def _():
        h_ref[...] = jnp.zeros_like(h_ref)
        act_ref[...] = jnp.zeros_like(act_ref)

    @pl.when(e2 == 0)
    def _():
        acc_ref[...] = jnp.zeros_like(acc_ref)

    @pl.when(e1 == 0)
    def _():
        for h in range(PEER_HEADS):
            taub_ref[h] = jnp.broadcast_to(tauc_ref[h], (SUBLANES, tm))

    h_ref[new] = _dot(u_ref[...], xt_ref[...])

    def piece(k, carry):
        rows = pl.ds(pl.multiple_of(k * db, db), db)
        acc_ref[rows, :] += _dot(vt_ref[rows, :], act_ref[old])
        for q in range(gp):
            kq = k * gp + q
            for h in range(PEER_HEADS):
                s1b_ref[q, h] = jnp.broadcast_to(s1c_ref[h, pl.ds(e1 * ni + kq, 1), :], (SUBLANES, tm))
            for jb in range(nk // rb):
                halves = []
                for j0 in (jb * rb, jb * rb + SUBLANES):
                    gate = None
                    for h in range(PEER_HEADS):
                        a = s1b_ref[q, h] + s2_ref[h, j0:j0 + SUBLANES, :]
                        gh = jnp.where(a >= taub_ref[h], jnp.exp2(a), 0.0)
                        gate = gh if gate is None else gate + gh
                    halves.append(gate)
                r0 = pl.multiple_of(kq * nk, nk) + jb * rb
                pre = h_ref[old, pl.ds(r0, rb), :]
                act = 0.5 * pre * (1.0 + lax.erf(pre * (1.0 / math.sqrt(2.0))))
                act_ref[new, pl.ds(r0, rb), :] = (act * jnp.concatenate(halves, axis=0)).astype(BF16)
        return carry

    lax.fori_loop(0, ni // gp, piece, 0)

    @pl.when(e2 == ne - 1)
    def _():
        o_ref[...] = acc_ref[...].astype(o_ref.dtype)


def _peer_dense(xt, u16, vt16, layer, s1c, s2, tauc, tm=512, te=512, groups_per_piece=1):
    d, t = xt.shape
    nexp = u16.shape[1]
    tm, te = _tile(t, tm), _tile(nexp, te)
    assert te % (PEER_NKEYS * groups_per_piece) == 0 and d % (te // PEER_NKEYS) == 0
    ne = nexp // te
    npairs = (t // tm) * ne
    pair = lambda g, lag: jnp.clip(g - lag, 0, npairs - 1)
    sc_spec = pl.BlockSpec((PEER_HEADS, PEER_NKEYS, tm), lambda g: (0, 0, pair(g, 1) // ne))
    row_spec = pl.BlockSpec((PEER_HEADS, 1, tm), lambda g: (0, 0, pair(g, 1) // ne))
    return pl.pallas_call(
        functools.partial(_peer_dense_kernel, te=te, ne=ne),
        grid=(npairs + 2,),
        in_specs=[pl.BlockSpec((d, tm), lambda g: (0, pair(g, 0) // ne)),
                  pl.BlockSpec((None, te, d), lambda g: (layer, pair(g, 0) % ne, 0)),
                  pl.BlockSpec((None, d, te), lambda g: (layer, 0, pair(g, 2) % ne)),
                  sc_spec, sc_spec, row_spec],
        out_specs=pl.BlockSpec((d, tm), lambda g: (0, pair(g, 2) // ne), pipeline_mode=pl.Buffered(1)),
        out_shape=jax.ShapeDtypeStruct((d, t), BF16),
        scratch_shapes=[pltpu.VMEM((2, te, tm), F32),
                        pltpu.VMEM((2, te, tm), BF16),
                        pltpu.VMEM((groups_per_piece, PEER_HEADS, SUBLANES, tm), F32),
                        pltpu.VMEM((PEER_HEADS, SUBLANES, tm), F32),
                        pltpu.VMEM((d, tm), F32)],
        compiler_params=_cparams("arbitrary"),
        name="peer_dense",
    )(xt, u16, vt16, s1c, s2, tauc)


def _peer_ffn(xt, wqt16, subkeys, u16, vt16, layer):
    qt = _matmul(wqt16, xt, BF16, 1024, 1024, a_layer=layer)
    s1c, s2, tauc = _peer_topk(qt, subkeys.astype(BF16))
    return _peer_dense(xt, u16, vt16, layer, s1c, s2, tauc)


def kernel(x, ssd_in_proj, ssd_conv_w, ssd_conv_b, ssd_dt_bias, ssd_A_log, ssd_D, ssd_norm_w,
           ssd_out_proj, sc_in_proj, sc_conv_w, sc_out_proj, peer_wq, peer_subkeys, peer_u,
           peer_v, ln1_g, ln1_b, ln2_g, ln2_b):
    batch, seq, d = x.shape
    depth = peer_wq.shape[0]
    alpha = (2 * depth) ** 0.25
    x32 = x.reshape(batch * seq, d).astype(F32)
    x16 = x32.astype(BF16)
    ssd_out16 = ssd_out_proj.astype(BF16)
    wqt16 = jnp.swapaxes(peer_wq, 1, 2).astype(BF16)
    u16 = peer_u.astype(BF16)
    vt16 = jnp.swapaxes(peer_v, 1, 2).astype(BF16)
    for i in range(depth):
        j = i // 2
        if i % 2 == 0:
            mix = _ssd_mixer(x16, ssd_in_proj, ssd_out16, j, ssd_conv_w[j], ssd_conv_b[j], ssd_dt_bias[j],
                             ssd_A_log[j], ssd_D[j], ssd_norm_w[j], batch, seq)
        else:
            mix = _shortconv_mixer(x16, sc_in_proj, sc_out_proj, j, sc_conv_w[j], batch, seq)
        x32, xt16 = _residual_ln(x32, mix, ln1_g[i], ln1_b[i], alpha, transposed=False)
        ffn_t = _peer_ffn(xt16, wqt16, peer_subkeys[i], u16, vt16, i)
        x32, x16 = _residual_ln(x32, ffn_t, ln2_g[i], ln2_b[i], alpha, transposed=True)
    return x32.reshape(batch, seq, d).astype(x.dtype)
```
